```python
import math
import jax, jax.numpy as jnp
from jax import lax
import numpy as np

D_MODEL = 1024
BATCH = 8
SEQ = 8192
DEPTH = 4

CHUNK = 64
N_LEFT_CHUNKS = 8
LEFT = N_LEFT_CHUNKS * CHUNK
BAND = (N_LEFT_CHUNKS + 1) * CHUNK
ATTN_HEADS = 8
HEAD_DIM = 64
D_ATTN = ATTN_HEADS * HEAD_DIM
MAX_REL = 256
N_REL = 2 * MAX_REL + 1
D_CONV = D_MODEL - D_ATTN
CONV_WIDTH = 31
N_BRANCHES = 2
D_IN = 2 * D_CONV + 3 * D_ATTN + N_BRANCHES * D_MODEL
N_EXPERTS = 16
N_GROUPS = 4
EXPERTS_PER_GROUP = N_EXPERTS // N_GROUPS
TOP_K = 2
D_FF_EXPERT = 512
ALPHA = (2.0 * DEPTH) ** 0.25
BETA = (8.0 * DEPTH) ** -0.25
LN_EPS = 1e-5

kernel_name = "hybrid_conv_chunkattn_grouped_moe_deepnorm"


def layer_norm(x, g, b):
    xf = x.astype(jnp.float32)
    mu = jnp.mean(xf, axis=-1, keepdims=True)
    var = jnp.mean(jnp.square(xf - mu), axis=-1, keepdims=True)
    return ((xf - mu) * lax.rsqrt(var + LN_EPS)).astype(x.dtype) * g + b


def conv_branch(glu_in, w_dw, b_dw, ln_g, ln_b, w_out):
    a, gate = jnp.split(glu_in, 2, axis=-1)
    u = a * jax.nn.sigmoid(gate)
    u = lax.conv_general_dilated(
        u, w_dw[:, None, :].astype(u.dtype),
        window_strides=(1,),
        padding=((CONV_WIDTH - 1, 0),),
        dimension_numbers=("NWC", "WIO", "NWC"),
        feature_group_count=D_CONV) + b_dw
    u = jax.nn.silu(layer_norm(u, ln_g, ln_b))
    return u @ w_out


def rel_index():
    i = jnp.arange(CHUNK)[:, None]
    j = jnp.arange(BAND)[None, :]
    rel = i - j + LEFT
    return jnp.clip(rel, -MAX_REL, MAX_REL) + MAX_REL


def chunked_attention(q, k, v, rel_bias):
    b, s = q.shape[0], q.shape[1]
    n_chunks = s // CHUNK
    qc = q.reshape(b, n_chunks, CHUNK, ATTN_HEADS, HEAD_DIM).transpose(1, 0, 3, 2, 4)
    kp = jnp.pad(k, ((0, 0), (LEFT, 0), (0, 0), (0, 0)))
    vp = jnp.pad(v, ((0, 0), (LEFT, 0), (0, 0), (0, 0)))
    bias = rel_bias[:, rel_index()].astype(jnp.float32)
    key_slot = jnp.arange(BAND)
    scale = HEAD_DIM ** -0.5

    def one_chunk(args):
        q_blk, c = args
        start = c * CHUNK
        k_blk = lax.dynamic_slice_in_dim(kp, start, BAND, axis=1)
        v_blk = lax.dynamic_slice_in_dim(vp, start, BAND, axis=1)
        scores = jnp.einsum("bhqd,bkhd->bhqk", q_blk, k_blk,
                            preferred_element_type=jnp.float32) * scale + bias
        valid = (start + key_slot) >= LEFT
        scores = jnp.where(valid, scores, -jnp.inf)
        p = jax.nn.softmax(scores, axis=-1).astype(v_blk.dtype)
        return jnp.einsum("bhqk,bkhd->bqhd", p, v_blk)

    out = lax.map(one_chunk, (qc, jnp.arange(n_chunks)))
    return out.transpose(1, 0, 2, 3, 4).reshape(b, s, D_ATTN)


def grouped_moe(h, w_router, b_router, w_gate_up, w_down):
    b, s, d = h.shape
    t = h.reshape(-1, d)
    logits = (t @ w_router + b_router).astype(jnp.float32)
    probs = jax.nn.softmax(logits, axis=-1)
    grouped = probs.reshape(-1, N_GROUPS, EXPERTS_PER_GROUP)
    top_vals, top_idx = lax.top_k(grouped, TOP_K)
    group_sel = jnp.argmax(jnp.sum(top_vals, axis=-1), axis=-1)
    sel_vals = jnp.take_along_axis(top_vals, group_sel[:, None, None], axis=1)[:, 0]
    sel_idx = (jnp.take_along_axis(top_idx, group_sel[:, None, None], axis=1)[:, 0]
               + group_sel[:, None] * EXPERTS_PER_GROUP)
    weights = sel_vals / jnp.sum(sel_vals, axis=-1, keepdims=True)
    combine = jnp.einsum("tk,tke->te", weights,
                         jax.nn.one_hot(sel_idx, N_EXPERTS, dtype=jnp.float32)).astype(t.dtype)
    out = jnp.zeros_like(t)
    for e in range(N_EXPERTS):
        g, u = jnp.split(t @ w_gate_up[e], 2, axis=-1)
        out = out + combine[:, e:e + 1] * ((jax.nn.silu(g) * u) @ w_down[e])
    return out.reshape(b, s, d)


def setup_inputs(seed: int = 0) -> dict:
    key = jax.random.key(seed)
    ks = jax.random.split(key, 24)
    L, D = DEPTH, D_MODEL
    nrm = lambda k, shape, s: jax.random.normal(k, shape, jnp.float32) * s
    return {
        "x": nrm(ks[0], (BATCH, SEQ, D), 1.0),
        "w_in": nrm(ks[1], (L, D, D_IN), D ** -0.5),
        "b_in": nrm(ks[2], (L, D_IN), 0.02),
        "w_dw": nrm(ks[3], (L, CONV_WIDTH, D_CONV), CONV_WIDTH ** -0.5),
        "b_dw": nrm(ks[4], (L, D_CONV), 0.02),
        "ln_conv_g": 1.0 + nrm(ks[5], (L, D_CONV), 0.02),
        "ln_conv_b": nrm(ks[6], (L, D_CONV), 0.02),
        "w_conv_out": nrm(ks[7], (L, D_CONV, D), D_CONV ** -0.5),
        "rel_bias": nrm(ks[8], (L, ATTN_HEADS, N_REL), 0.1),
        "w_attn_out": nrm(ks[9], (L, D_ATTN, D), D_ATTN ** -0.5),
        "w_o": nrm(ks[10], (L, D, D), BETA * D ** -0.5),
        "b_o": nrm(ks[11], (L, D), 0.02),
        "ln1_g": 1.0 + nrm(ks[12], (L, D), 0.02),
        "ln1_b": nrm(ks[13], (L, D), 0.02),
        "w_router": nrm(ks[14], (D, N_EXPERTS), D ** -0.5),
        "b_router": nrm(ks[15], (N_EXPERTS,), 0.01),
        "w_gate_up": nrm(ks[16], (L, N_EXPERTS, D, 2 * D_FF_EXPERT), D ** -0.5),
        "w_down": nrm(ks[17], (L, N_EXPERTS, D_FF_EXPERT, D), BETA * D_FF_EXPERT ** -0.5),
        "ln2_g": 1.0 + nrm(ks[18], (L, D), 0.02),
        "ln2_b": nrm(ks[19], (L, D), 0.02),
    }


def reference(x, w_in, b_in, w_dw, b_dw, ln_conv_g, ln_conv_b, w_conv_out, rel_bias,
              w_attn_out, w_o, b_o, ln1_g, ln1_b, w_router, b_router, w_gate_up,
              w_down, ln2_g, ln2_b):
    b, s, _ = x.shape
    splits = [2 * D_CONV, 2 * D_CONV + D_ATTN, 2 * D_CONV + 2 * D_ATTN, 2 * D_CONV + 3 * D_ATTN]
    for l in range(DEPTH):
        proj = x @ w_in[l] + b_in[l]
        glu_in, q, k, v, gate_logits = jnp.split(proj, splits, axis=-1)
        y_conv = conv_branch(glu_in, w_dw[l], b_dw[l], ln_conv_g[l], ln_conv_b[l], w_conv_out[l])
        heads = lambda t: t.reshape(b, s, ATTN_HEADS, HEAD_DIM)
        y_attn = chunked_attention(heads(q), heads(k), heads(v), rel_bias[l]) @ w_attn_out[l]
        g_conv, g_attn = jnp.split(jax.nn.sigmoid(gate_logits), N_BRANCHES, axis=-1)
        mixed = (g_conv * y_conv + g_attn * y_attn) @ w_o[l] + b_o[l]
        x = layer_norm(ALPHA * x + mixed, ln1_g[l], ln1_b[l])
        x = layer_norm(ALPHA * x + grouped_moe(x, w_router, b_router, w_gate_up[l], w_down[l]),
                       ln2_g[l], ln2_b[l])
    return x
```

```python
import functools

import jax
import jax.numpy as jnp
import numpy as np
from jax import lax
from jax.experimental import pallas as pl
from jax.experimental.pallas import tpu as pltpu

f32 = jnp.float32
bf16 = jnp.bfloat16
i32 = jnp.int32

D_MODEL = 1024
CHUNK = 64
N_LEFT_CHUNKS = 8
LEFT = N_LEFT_CHUNKS * CHUNK
BAND = LEFT + CHUNK
ATTN_HEADS = 8
HEAD_DIM = 64
D_ATTN = ATTN_HEADS * HEAD_DIM
MAX_REL = 256
D_CONV = D_MODEL - D_ATTN
CONV_WIDTH = 31
D_IN = 2 * D_CONV + 3 * D_ATTN + 2 * D_MODEL
N_EXPERTS = 16
N_GROUPS = 4
EXPERTS_PER_GROUP = N_EXPERTS // N_GROUPS
D_FF_EXPERT = 512
LN_EPS = 1e-5

PAIRS = [(a, b) for a in range(EXPERTS_PER_GROUP) for b in range(a + 1, EXPERTS_PER_GROUP)]
N_PAIRS = len(PAIRS)
N_BUCKETS = N_GROUPS * N_PAIRS
BUCKET_ROWS = 32

LANES = 128
TOKEN_TILE = 512
EXPERT_TILE = 256
HALO_ROWS = 32
CONV_ROWS = 64
AUG = D_MODEL + LANES
VMEM_LIMIT = 56 * 1024 * 1024
NEG_BIG = -1e30


def _layer_norm(x, g, b):
    mu = jnp.mean(x, axis=-1, keepdims=True)
    xc = x - mu
    var = jnp.mean(xc * xc, axis=-1, keepdims=True)
    return xc * lax.rsqrt(var + LN_EPS) * g + b


def _sigmoid(x):
    return 1.0 / (1.0 + jnp.exp(-x))


def _proj_kernel(x_ref, w_ref, b_ref, u_ref, q_ref, k_ref, v_ref, g_ref):
    x = x_ref[...].astype(bf16)

    def seg(lo, hi):
        return jnp.dot(x, w_ref[:, lo:hi], preferred_element_type=f32) + b_ref[:, lo:hi]

    o = 0
    a = seg(o, o + D_CONV)
    gate = seg(o + D_CONV, o + 2 * D_CONV)
    u_ref[...] = (a * _sigmoid(gate)).astype(bf16)
    o += 2 * D_CONV
    q_ref[...] = (seg(o, o + D_ATTN) * (HEAD_DIM ** -0.5)).astype(bf16)
    o += D_ATTN
    k_ref[...] = seg(o, o + D_ATTN).astype(bf16)
    o += D_ATTN
    v_ref[...] = seg(o, o + D_ATTN).astype(bf16)
    o += D_ATTN
    g_ref[...] = _sigmoid(seg(o, o + 2 * D_MODEL)).astype(bf16)


def _proj(x, w_in, b_in):
    t = x.shape[0]
    tm = TOKEN_TILE
    row = lambda n: pl.BlockSpec((tm, n), lambda i: (i, 0))
    const = lambda shape: pl.BlockSpec(shape, lambda i: (0,) * len(shape))
    return pl.pallas_call(
        _proj_kernel,
        grid=(t // tm,),
        in_specs=[row(D_MODEL), const((D_MODEL, D_IN)), const((1, D_IN))],
        out_specs=[row(D_CONV), row(D_ATTN), row(D_ATTN), row(D_ATTN), row(2 * D_MODEL)],
        out_shape=[jax.ShapeDtypeStruct((t, n), bf16)
                   for n in (D_CONV, D_ATTN, D_ATTN, D_ATTN, 2 * D_MODEL)],
        compiler_params=pltpu.CompilerParams(
            dimension_semantics=("arbitrary",), vmem_limit_bytes=VMEM_LIMIT),
        name="proj",
    )(x, w_in, b_in)


def _route(x1, wrt_ref, br_ref, tri_ref, run_ref):
    ts = x1.shape[0]
    logits = lax.dot_general(wrt_ref[...], x1, (((1,), (1,)), ((), ())),
                             precision=lax.Precision.HIGHEST,
                             preferred_element_type=f32) + br_ref[...]
    mx = jnp.max(logits, axis=0, keepdims=True)
    ex = jnp.exp(logits - mx)
    probs = ex / jnp.sum(ex, axis=0, keepdims=True)

    best = None
    for g in range(N_GROUPS):
        r = [probs[g * EXPERTS_PER_GROUP + j:g * EXPERTS_PER_GROUP + j + 1, :]
             for j in range(EXPERTS_PER_GROUP)]
        v1 = jnp.maximum(jnp.maximum(r[0], r[1]), jnp.maximum(r[2], r[3]))
        i1 = jnp.where(r[0] == v1, 0, jnp.where(r[1] == v1, 1, jnp.where(r[2] == v1, 2, 3)))
        rm = [jnp.where(i1 == j, -1.0, r[j]) for j in range(EXPERTS_PER_GROUP)]
        v2 = jnp.maximum(jnp.maximum(rm[0], rm[1]), jnp.maximum(rm[2], rm[3]))
        i2 = jnp.where(rm[0] == v2, 0, jnp.where(rm[1] == v2, 1, jnp.where(rm[2] == v2, 2, 3)))
        cand = (v1 + v2, v1, v2, i1, i2, jnp.full_like(i1, g))
        if best is None:
            best = cand
        else:
            take = cand[0] > best[0]
            best = tuple(jnp.where(take, c, b) for c, b in zip(cand, best))
    _, v1, v2, i1, i2, grp = best
    den = v1 + v2
    w1 = v1 / den
    w2 = v2 / den
    first_is_lo = i1 < i2
    lo = jnp.where(first_is_lo, i1, i2)
    hi = jnp.where(first_is_lo, i2, i1)
    w_lo = jnp.where(first_is_lo, w1, w2)
    w_hi = jnp.where(first_is_lo, w2, w1)
    pair_base = jnp.where(lo == 0, 0, jnp.where(lo == 1, 3, 5))
    bucket = grp * N_PAIRS + pair_base + (hi - lo - 1)

    rows = lax.broadcasted_iota(i32, (BUCKET_ROWS, ts), 0)
    onehot = (rows == bucket).astype(f32)
    before = jnp.dot(onehot.astype(bf16), tri_ref[...], preferred_element_type=f32)
    rank = jnp.sum(onehot * (before + run_ref[:, 0:1]), axis=0, keepdims=True)
    run_ref[...] = run_ref[...] + jnp.sum(onehot, axis=1, keepdims=True)
    return bucket, rank.astype(i32), w_lo, w_hi


def _mix_kernel(alpha,
                u_ref, uh_ref, q_ref, kc_ref, kp_ref, vc_ref, vp_ref, g_ref, x_ref,
                wdw_ref, bdw_ref, lcg_ref, lcb_ref, wco_ref, bias_ref, wao_ref,
                wo_ref, bo_ref, l1g_ref, l1b_ref, wrt_ref, br_ref,
                xa_ref, bucket_ref, rank_ref, counts_ref,
                uext, kbuf, vbuf, abuf, cbuf, tri_ref, run_ref):
    b = pl.program_id(0)
    i = pl.program_id(1)
    ts = u_ref.shape[0]

    @pl.when(jnp.logical_and(b == 0, i == 0))
    def _():
        run_ref[...] = jnp.zeros_like(run_ref)
        s = lax.broadcasted_iota(i32, (ts, ts), 0)
        t = lax.broadcasted_iota(i32, (ts, ts), 1)
        tri_ref[...] = jnp.where(s < t, 1.0, 0.0).astype(bf16)

    halo = uh_ref[...].astype(f32)
    uext[0:HALO_ROWS, :] = jnp.where(i == 0, 0.0, halo)
    uext[HALO_ROWS:HALO_ROWS + ts, :] = u_ref[...].astype(f32)
    bdw = bdw_ref[...]
    lcg = lcg_ref[...]
    lcb = lcb_ref[...]
    first_tap = HALO_ROWS - (CONV_WIDTH - 1)
    for rc in range(ts // CONV_ROWS):
        r0 = rc * CONV_ROWS
        acc = jnp.zeros((CONV_ROWS, D_CONV), f32)
        for w in range(CONV_WIDTH):
            acc = acc + uext[r0 + first_tap + w:r0 + first_tap + w + CONV_ROWS, :] * wdw_ref[w:w + 1, :]
        y = _layer_norm(acc + bdw, lcg, lcb)
        y = y * _sigmoid(y)
        cbuf[r0:r0 + CONV_ROWS, :] = y.astype(bf16)

    kbuf[0:ts, :] = kp_ref[...]
    kbuf[ts:2 * ts, :] = kc_ref[...]
    vbuf[0:ts, :] = vp_ref[...]
    vbuf[ts:2 * ts, :] = vc_ref[...]
    col = lax.broadcasted_iota(i32, (CHUNK, BAND), 1)

    def chunk_body(c, carry):
        r0 = pl.multiple_of(c * CHUNK, CHUNK)
        valid = col >= (LEFT - i * ts - c * CHUNK)
        for hp in range(ATTN_HEADS // 2):
            lanes = slice(hp * 2 * HEAD_DIM, (hp + 1) * 2 * HEAD_DIM)
            q2 = q_ref[pl.ds(r0, CHUNK), lanes]
            k2 = kbuf[pl.ds(r0, BAND), lanes]
            v2 = vbuf[pl.ds(r0, BAND), lanes]
            outs = []
            for hh in range(2):
                hs = slice(hh * HEAD_DIM, (hh + 1) * HEAD_DIM)
                s = lax.dot_general(q2[:, hs], k2[:, hs], (((1,), (1,)), ((), ())),
                                    preferred_element_type=f32)
                s = s + bias_ref[hp * 2 + hh]
                s = jnp.where(valid, s, NEG_BIG)
                m = jnp.max(s, axis=-1, keepdims=True)
                p = jnp.exp(s - m)
                den = jnp.sum(p, axis=-1, keepdims=True)
                o = jnp.dot(p.astype(bf16), v2[:, hs], preferred_element_type=f32)
                outs.append(o / den)
            abuf[pl.ds(r0, CHUNK), lanes] = jnp.concatenate(outs, axis=1).astype(bf16)
        return carry

    lax.fori_loop(0, ts // CHUNK, chunk_body, 0)

    y_conv = jnp.dot(cbuf[...], wco_ref[...], preferred_element_type=f32)
    y_attn = jnp.dot(abuf[...], wao_ref[...], preferred_element_type=f32)
    merged = (g_ref[:, 0:D_MODEL].astype(f32) * y_conv
              + g_ref[:, D_MODEL:2 * D_MODEL].astype(f32) * y_attn)
    mixed = jnp.dot(merged.astype(bf16), wo_ref[...], preferred_element_type=f32) + bo_ref[...]
    x1 = _layer_norm(alpha * x_ref[...] + mixed, l1g_ref[...], l1b_ref[...])

    bucket, rank, w_lo, w_hi = _route(x1, wrt_ref, br_ref, tri_ref, run_ref)
    bucket_ref[0] = bucket
    rank_ref[0] = rank
    counts_ref[...] = run_ref[...]
    wrows = jnp.concatenate([w_lo, w_hi, jnp.zeros((LANES - 2, ts), f32)], axis=0)
    xa_ref[:, 0:D_MODEL] = x1
    xa_ref[:, D_MODEL:AUG] = wrows.T


def _mix(alpha, nb, ns, u, q, k, v, g, x, wdw, bdw, lcg, lcb, wco, bias, wao, wo, bo, l1g, l1b,
         wrt, br):
    ts = TOKEN_TILE
    nst = ns // ts
    t = nb * ns
    halo_per_tile = ts // HALO_ROWS
    cur = lambda n: pl.BlockSpec((ts, n), lambda b, i: (b * nst + i, 0))
    prev = lambda n: pl.BlockSpec((ts, n), lambda b, i: (jnp.maximum(b * nst + i - 1, 0), 0))
    const = lambda shape: pl.BlockSpec(shape, lambda b, i: (0,) * len(shape))
    meta = pl.BlockSpec((1, 1, ts), lambda b, i: (b * nst + i, 0, 0))
    in_specs = [
        cur(D_CONV),
        pl.BlockSpec((HALO_ROWS, D_CONV),
                     lambda b, i: (jnp.maximum((b * nst + i) * halo_per_tile - 1, 0), 0)),
        cur(D_ATTN), cur(D_ATTN), prev(D_ATTN), cur(D_ATTN), prev(D_ATTN),
        cur(2 * D_MODEL), cur(D_MODEL),
        const((HALO_ROWS, D_CONV)), const((1, D_CONV)), const((1, D_CONV)), const((1, D_CONV)),
        const((D_CONV, D_MODEL)), const((ATTN_HEADS, CHUNK, BAND)), const((D_ATTN, D_MODEL)),
        const((D_MODEL, D_MODEL)), const((1, D_MODEL)), const((1, D_MODEL)), const((1, D_MODEL)),
        const((N_EXPERTS, D_MODEL)), const((N_EXPERTS, 1)),
    ]
    out_specs = [
        pl.BlockSpec((ts, AUG), lambda b, i: (b * nst + i, 0)),
        meta, meta,
        pl.BlockSpec((BUCKET_ROWS, LANES), lambda b, i: (0, 0)),
    ]
    out_shape = [
        jax.ShapeDtypeStruct((t, AUG), f32),
        jax.ShapeDtypeStruct((t // ts, 1, ts), i32),
        jax.ShapeDtypeStruct((t // ts, 1, ts), i32),
        jax.ShapeDtypeStruct((BUCKET_ROWS, LANES), f32),
    ]
    scratch = [
        pltpu.VMEM((HALO_ROWS + ts, D_CONV), f32),
        pltpu.VMEM((2 * ts, D_ATTN), bf16),
        pltpu.VMEM((2 * ts, D_ATTN), bf16),
        pltpu.VMEM((ts, D_ATTN), bf16),
        pltpu.VMEM((ts, D_CONV), bf16),
        pltpu.VMEM((ts, ts), bf16),
        pltpu.VMEM((BUCKET_ROWS, LANES), f32),
    ]
    return pl.pallas_call(
        functools.partial(_mix_kernel, alpha),
        grid=(nb, nst),
        in_specs=in_specs, out_specs=out_specs, out_shape=out_shape,
        scratch_shapes=scratch,
        compiler_params=pltpu.CompilerParams(
            dimension_semantics=("arbitrary", "arbitrary"), vmem_limit_bytes=VMEM_LIMIT),
        name="mix",
    )(u, u, q, k, k, v, v, g, x, wdw, bdw, lcg, lcb, wco, bias, wao, wo, bo, l1g, l1b, wrt, br)


def _row_copy(src, src_row, dst, dst_row, sem):
    return pltpu.make_async_copy(src.at[pl.ds(src_row, 1)], dst.at[pl.ds(dst_row, 1)], sem)


def _scatter_kernel(pos_ref, xa_hbm, xs_in_hbm, xs_hbm, sem):
    del xs_in_hbm
    ts = pos_ref.shape[-1]
    base = pl.program_id(0) * ts

    def start(r, c):
        _row_copy(xa_hbm, base + r, xs_hbm, pos_ref[0, 0, r], sem).start()
        return c

    def wait(r, c):
        _row_copy(xa_hbm, base + r, xs_hbm, pos_ref[0, 0, r], sem).wait()
        return c

    lax.fori_loop(0, ts, start, 0)
    lax.fori_loop(0, ts, wait, 0)


def _scatter(pos, xa, n_rows):
    nt, _, ts = pos.shape
    xs0 = jnp.zeros((n_rows, AUG), f32)
    return pl.pallas_call(
        _scatter_kernel,
        grid=(nt,),
        in_specs=[pl.BlockSpec((1, 1, ts), lambda i: (i, 0, 0), memory_space=pltpu.SMEM),
                  pl.BlockSpec(memory_space=pl.ANY),
                  pl.BlockSpec(memory_space=pl.ANY)],
        out_specs=pl.BlockSpec(memory_space=pl.ANY),
        out_shape=jax.ShapeDtypeStruct((n_rows, AUG), f32),
        scratch_shapes=[pltpu.SemaphoreType.DMA(())],
        input_output_aliases={2: 0},
        compiler_params=pltpu.CompilerParams(
            dimension_semantics=("arbitrary",), has_side_effects=True),
        name="scatter",
    )(pos, xa, xs0)


def _expert_kernel(elo_ref, ehi_ref, valid_ref, xs_ref, wgl_ref, wgh_ref, wdl_ref, wdh_ref, ys_ref):
    j = pl.program_id(0)

    @pl.when(valid_ref[j] == 1)
    def _():
        x = xs_ref[:, 0:D_MODEL].astype(bf16)
        y = None
        for e, (wg_ref, wd_ref) in enumerate(((wgl_ref, wdl_ref), (wgh_ref, wdh_ref))):
            gu = jnp.dot(x, wg_ref[...], preferred_element_type=f32)
            gate = gu[:, 0:D_FF_EXPERT]
            up = gu[:, D_FF_EXPERT:2 * D_FF_EXPERT]
            h = (gate * _sigmoid(gate)) * up
            d = jnp.dot(h.astype(bf16), wd_ref[...], preferred_element_type=f32)
            d = xs_ref[:, D_MODEL + e:D_MODEL + e + 1] * d
            y = d if y is None else y + d
        ys_ref[...] = y

    @pl.when(valid_ref[j] == 0)
    def _():
        ys_ref[...] = jnp.zeros_like(ys_ref)


def _experts(e_lo, e_hi, valid, xs, wgu, wd):
    n_rows = xs.shape[0]
    tm = EXPERT_TILE
    grid_spec = pltpu.PrefetchScalarGridSpec(
        num_scalar_prefetch=3,
        grid=(n_rows // tm,),
        in_specs=[
            pl.BlockSpec((tm, AUG), lambda j, lo, hi, ok: (j, 0)),
            pl.BlockSpec((None, D_MODEL, 2 * D_FF_EXPERT), lambda j, lo, hi, ok: (lo[j], 0, 0)),
            pl.BlockSpec((None, D_MODEL, 2 * D_FF_EXPERT), lambda j, lo, hi, ok: (hi[j], 0, 0)),
            pl.BlockSpec((None, D_FF_EXPERT, D_MODEL), lambda j, lo, hi, ok: (lo[j], 0, 0)),
            pl.BlockSpec((None, D_FF_EXPERT, D_MODEL), lambda j, lo, hi, ok: (hi[j], 0, 0)),
        ],
        out_specs=pl.BlockSpec((tm, D_MODEL), lambda j, lo, hi, ok: (j, 0)),
    )
    return pl.pallas_call(
        _expert_kernel,
        grid_spec=grid_spec,
        out_shape=jax.ShapeDtypeStruct((n_rows, D_MODEL), f32),
        compiler_params=pltpu.CompilerParams(
            dimension_semantics=("arbitrary",), vmem_limit_bytes=VMEM_LIMIT),
        name="experts",
    )(e_lo, e_hi, valid, xs, wgu, wgu, wd, wd)


def _gather_kernel(alpha, pos_ref, xa_ref, g_ref, b_ref, ys_hbm, out_ref, buf, sem):
    ts = pos_ref.shape[-1]

    def start(r, c):
        _row_copy(ys_hbm, pos_ref[0, 0, r], buf, r, sem).start()
        return c

    def wait(r, c):
        _row_copy(ys_hbm, pos_ref[0, 0, r], buf, r, sem).wait()
        return c

    lax.fori_loop(0, ts, start, 0)
    lax.fori_loop(0, ts, wait, 0)
    out_ref[...] = _layer_norm(alpha * xa_ref[...] + buf[...], g_ref[...], b_ref[...])


def _gather(alpha, pos, xa, ys, g, b):
    nt, _, ts = pos.shape
    t = nt * ts
    return pl.pallas_call(
        functools.partial(_gather_kernel, alpha),
        grid=(nt,),
        in_specs=[pl.BlockSpec((1, 1, ts), lambda i: (i, 0, 0), memory_space=pltpu.SMEM),
                  pl.BlockSpec((ts, D_MODEL), lambda i: (i, 0)),
                  pl.BlockSpec((1, D_MODEL), lambda i: (0, 0)),
                  pl.BlockSpec((1, D_MODEL), lambda i: (0, 0)),
                  pl.BlockSpec(memory_space=pl.ANY)],
        out_specs=pl.BlockSpec((ts, D_MODEL), lambda i: (i, 0)),
        out_shape=jax.ShapeDtypeStruct((t, D_MODEL), f32),
        scratch_shapes=[pltpu.VMEM((ts, D_MODEL), f32), pltpu.SemaphoreType.DMA(())],
        compiler_params=pltpu.CompilerParams(
            dimension_semantics=("arbitrary",), vmem_limit_bytes=VMEM_LIMIT),
        name="gather",
    )(pos, xa, g, b, ys)


def _bucket_tables():
    lo = np.array([g * EXPERTS_PER_GROUP + a for g in range(N_GROUPS) for a, _ in PAIRS], np.int32)
    hi = np.array([g * EXPERTS_PER_GROUP + b for g in range(N_GROUPS) for _, b in PAIRS], np.int32)
    return jnp.asarray(lo), jnp.asarray(hi)


def _rel_index():
    qi = np.arange(CHUNK)[:, None]
    kj = np.arange(BAND)[None, :]
    return np.clip(qi - kj + LEFT, -MAX_REL, MAX_REL) + MAX_REL


def kernel(x, w_in, b_in, w_dw, b_dw, ln_conv_g, ln_conv_b, w_conv_out, rel_bias, w_attn_out,
           w_o, b_o, ln1_g, ln1_b, w_router, b_router, w_gate_up, w_down, ln2_g, ln2_b):
    nb, ns, d = x.shape
    depth = w_in.shape[0]
    assert d == D_MODEL and ns % TOKEN_TILE == 0 and TOKEN_TILE >= LEFT
    alpha = (2.0 * depth) ** 0.25
    t = nb * ns
    n_tiles = t // EXPERT_TILE + N_BUCKETS
    n_rows = n_tiles * EXPERT_TILE
    lo_tab, hi_tab = _bucket_tables()
    rel_idx = _rel_index()
    row = lambda a: a.reshape(1, -1)

    wrt = w_router.T
    br = b_router.reshape(N_EXPERTS, 1)
    h = x.reshape(t, d)
    for l in range(depth):
        u, q, k, v, g = _proj(h, w_in[l].astype(bf16), row(b_in[l]))
        wdw = jnp.zeros((HALO_ROWS, D_CONV), f32).at[:CONV_WIDTH].set(w_dw[l])
        bias = rel_bias[l][:, rel_idx]
        xa, bucket, rank, counts = _mix(
            alpha, nb, ns, u, q, k, v, g, h, wdw, row(b_dw[l]), row(ln_conv_g[l]),
            row(ln_conv_b[l]), w_conv_out[l].astype(bf16), bias, w_attn_out[l].astype(bf16),
            w_o[l].astype(bf16), row(b_o[l]), row(ln1_g[l]), row(ln1_b[l]), wrt, br)

        cnt = counts[:N_BUCKETS, 0].astype(i32)
        tiles = (cnt + EXPERT_TILE - 1) // EXPERT_TILE
        tile_end = jnp.cumsum(tiles)
        row_start = (tile_end - tiles) * EXPERT_TILE
        pos = row_start[bucket] + rank
        tile_ids = jnp.arange(n_tiles, dtype=i32)
        tile_bucket = jnp.minimum(
            jnp.searchsorted(tile_end, tile_ids, side="right").astype(i32), N_BUCKETS - 1)
        valid = (tile_ids < tile_end[-1]).astype(i32)

        xs = _scatter(pos, xa, n_rows)
        ys = _experts(lo_tab[tile_bucket], hi_tab[tile_bucket], valid, xs,
                      w_gate_up[l].astype(bf16), w_down[l].astype(bf16))
        h = _gather(alpha, pos, xa, ys, row(ln2_g[l]), row(ln2_b[l]))
    return h.reshape(nb, ns, d)
```

```python
import functools
import math

import jax
import jax.numpy as jnp
import numpy as np
from jax import lax
from jax.experimental import pallas as pl
from jax.experimental.pallas import tpu as pltpu

f32 = jnp.float32
bf16 = jnp.bfloat16
i32 = jnp.int32

D_MODEL = 1024
CHUNK = 64
N_LEFT_CHUNKS = 8
LEFT = N_LEFT_CHUNKS * CHUNK
BAND = LEFT + CHUNK
ATTN_HEADS = 8
HEAD_DIM = 64
D_ATTN = ATTN_HEADS * HEAD_DIM
MAX_REL = 256
D_CONV = D_MODEL - D_ATTN
CONV_WIDTH = 31
D_IN = 2 * D_CONV + 3 * D_ATTN + 2 * D_MODEL
N_EXPERTS = 16
N_GROUPS = 4
EXPERTS_PER_GROUP = N_EXPERTS // N_GROUPS
D_FF_EXPERT = 512
LN_EPS = 1e-5
LOG2E = math.log2(math.e)

PAIRS = [(a, b) for a in range(EXPERTS_PER_GROUP) for b in range(a + 1, EXPERTS_PER_GROUP)]
N_PAIRS = len(PAIRS)
N_BUCKETS = N_GROUPS * N_PAIRS
BUCKET_ROWS = 32

LANES = 128
SUBLANES = 8
TOKEN_TILE = 512
EXPERT_TILE = 256
HALO_ROWS = 32
CONV_ROWS = 64
AUG = D_MODEL + LANES
DMA_UNROLL = 8
VMEM_LIMIT = 56 * 1024 * 1024
NEG_BIG = -1e30


def _layer_norm(x, g, b):
    mu = jnp.mean(x, axis=-1, keepdims=True)
    xc = x - mu
    var = jnp.mean(xc * xc, axis=-1, keepdims=True)
    return xc * lax.rsqrt(var + LN_EPS) * g + b


def _sigmoid(x):
    return 1.0 / (1.0 + jnp.exp(-x))


def _row_copy(src, src_row, dst, dst_row, sem):
    return pltpu.make_async_copy(src.at[pl.ds(src_row, 1)], dst.at[pl.ds(dst_row, 1)], sem)


def _gather_rows(op, idx_ref, src_hbm, dst, sem):
    def body(r, c):
        getattr(_row_copy(src_hbm, idx_ref[0, 0, r], dst, r, sem), op)()
        return c
    lax.fori_loop(0, idx_ref.shape[-1], body, 0, unroll=DMA_UNROLL)


def _scatter_rows(op, idx_ref, src, dst_hbm, sem):
    def body(r, c):
        getattr(_row_copy(src, r, dst_hbm, idx_ref[0, 0, r], sem), op)()
        return c
    lax.fori_loop(0, idx_ref.shape[-1], body, 0, unroll=DMA_UNROLL)


def _gather_step(pos_ref, nxt_ref, src_hbm, buf, sem):
    i = pl.program_id(0)
    slot = lax.rem(i, 2)

    @pl.when(i == 0)
    def _():
        _gather_rows("start", pos_ref, src_hbm, buf.at[0], sem.at[0])

    @pl.when(i + 1 < pl.num_programs(0))
    def _():
        _gather_rows("start", nxt_ref, src_hbm, buf.at[1 - slot], sem.at[1 - slot])

    _gather_rows("wait", pos_ref, src_hbm, buf.at[slot], sem.at[slot])
    return slot


def _pos_specs(nt, ts):
    cur = pl.BlockSpec((1, 1, ts), lambda i: (i, 0, 0), memory_space=pltpu.SMEM)
    nxt = pl.BlockSpec((1, 1, ts), lambda i: (jnp.minimum(i + 1, nt - 1), 0, 0),
                       memory_space=pltpu.SMEM)
    return cur, nxt


def _proj_body(x, w_ref, b_ref, u_ref, q_ref, k_ref, v_ref, g_ref):
    x = x.astype(bf16)

    def seg(lo, hi):
        return jnp.dot(x, w_ref[:, lo:hi], preferred_element_type=f32) + b_ref[:, lo:hi]

    o = 0
    a = seg(o, o + D_CONV)
    gate = seg(o + D_CONV, o + 2 * D_CONV)
    u_ref[...] = (a * _sigmoid(gate)).astype(bf16)
    o += 2 * D_CONV
    q_ref[...] = (seg(o, o + D_ATTN) * (HEAD_DIM ** -0.5 * LOG2E)).astype(bf16)
    o += D_ATTN
    k_ref[...] = seg(o, o + D_ATTN).astype(bf16)
    o += D_ATTN
    v_ref[...] = seg(o, o + D_ATTN).astype(bf16)
    o += D_ATTN
    g_ref[...] = _sigmoid(seg(o, o + 2 * D_MODEL)).astype(bf16)


def _proj_kernel(x_ref, w_ref, b_ref, *out_refs):
    _proj_body(x_ref[...], w_ref, b_ref, *out_refs)


def _proj_gather_kernel(pos_ref, nxt_ref, ys_hbm, w_ref, b_ref, x_ref, *rest):
    out_refs, (buf, sem) = rest[:5], rest[5:]
    slot = _gather_step(pos_ref, nxt_ref, ys_hbm, buf, sem)
    x = buf[slot]
    x_ref[...] = x
    _proj_body(x, w_ref, b_ref, *out_refs)


_PROJ_WIDTHS = (D_CONV, D_ATTN, D_ATTN, D_ATTN, 2 * D_MODEL)


def _proj(x, w_in, b_in):
    t = x.shape[0]
    tm = TOKEN_TILE
    row = lambda n: pl.BlockSpec((tm, n), lambda i: (i, 0))
    const = lambda shape: pl.BlockSpec(shape, lambda i: (0,) * len(shape))
    return pl.pallas_call(
        _proj_kernel,
        grid=(t // tm,),
        in_specs=[row(D_MODEL), const((D_MODEL, D_IN)), const((1, D_IN))],
        out_specs=[row(n) for n in _PROJ_WIDTHS],
        out_shape=[jax.ShapeDtypeStruct((t, n), bf16) for n in _PROJ_WIDTHS],
        compiler_params=pltpu.CompilerParams(
            dimension_semantics=("arbitrary",), vmem_limit_bytes=VMEM_LIMIT),
        name="proj",
    )(x, w_in, b_in)


def _proj_gather(pos, ys, w_in, b_in):
    nt, _, tm = pos.shape
    t = nt * tm
    row = lambda n: pl.BlockSpec((tm, n), lambda i: (i, 0))
    const = lambda shape: pl.BlockSpec(shape, lambda i: (0,) * len(shape))
    cur, nxt = _pos_specs(nt, tm)
    return pl.pallas_call(
        _proj_gather_kernel,
        grid=(nt,),
        in_specs=[cur, nxt, pl.BlockSpec(memory_space=pl.ANY),
                  const((D_MODEL, D_IN)), const((1, D_IN))],
        out_specs=[row(D_MODEL)] + [row(n) for n in _PROJ_WIDTHS],
        out_shape=[jax.ShapeDtypeStruct((t, D_MODEL), f32)]
                  + [jax.ShapeDtypeStruct((t, n), bf16) for n in _PROJ_WIDTHS],
        scratch_shapes=[pltpu.VMEM((2, tm, D_MODEL), f32), pltpu.SemaphoreType.DMA((2,))],
        compiler_params=pltpu.CompilerParams(
            dimension_semantics=("arbitrary",), vmem_limit_bytes=VMEM_LIMIT),
        name="proj_gather",
    )(pos, pos, ys, w_in, b_in)


def _route(x1, wrt_ref, br_ref, tri_ref, run_ref):
    ts = x1.shape[0]
    logits = lax.dot_general(wrt_ref[...], x1, (((1,), (1,)), ((), ())),
                             precision=lax.Precision.HIGHEST,
                             preferred_element_type=f32) + br_ref[...]
    mx = jnp.max(logits, axis=0, keepdims=True)
    ex = jnp.exp(logits - mx)
    probs = ex / jnp.sum(ex, axis=0, keepdims=True)

    best = None
    for g in range(N_GROUPS):
        r = [probs[g * EXPERTS_PER_GROUP + j:g * EXPERTS_PER_GROUP + j + 1, :]
             for j in range(EXPERTS_PER_GROUP)]
        v1 = jnp.maximum(jnp.maximum(r[0], r[1]), jnp.maximum(r[2], r[3]))
        i1 = jnp.where(r[0] == v1, 0, jnp.where(r[1] == v1, 1, jnp.where(r[2] == v1, 2, 3)))
        rm = [jnp.where(i1 == j, -1.0, r[j]) for j in range(EXPERTS_PER_GROUP)]
        v2 = jnp.maximum(jnp.maximum(rm[0], rm[1]), jnp.maximum(rm[2], rm[3]))
        i2 = jnp.where(rm[0] == v2, 0, jnp.where(rm[1] == v2, 1, jnp.where(rm[2] == v2, 2, 3)))
        cand = (v1 + v2, v1, v2, i1, i2, jnp.full_like(i1, g))
        if best is None:
            best = cand
        else:
            take = cand[0] > best[0]
            best = tuple(jnp.where(take, c, b) for c, b in zip(cand, best))
    _, v1, v2, i1, i2, grp = best
    den = v1 + v2
    w1 = v1 / den
    w2 = v2 / den
    first_is_lo = i1 < i2
    lo = jnp.where(first_is_lo, i1, i2)
    hi = jnp.where(first_is_lo, i2, i1)
    w_lo = jnp.where(first_is_lo, w1, w2)
    w_hi = jnp.where(first_is_lo, w2, w1)
    pair_base = jnp.where(lo == 0, 0, jnp.where(lo == 1, 3, 5))
    bucket = grp * N_PAIRS + pair_base + (hi - lo - 1)

    rows = lax.broadcasted_iota(i32, (BUCKET_ROWS, ts), 0)
    onehot = (rows == bucket).astype(f32)
    before = jnp.dot(onehot.astype(bf16), tri_ref[...], preferred_element_type=f32)
    rank = jnp.sum(onehot * (before + run_ref[:, 0:1]), axis=0, keepdims=True)
    run_ref[...] = run_ref[...] + jnp.sum(onehot, axis=1, keepdims=True)
    return bucket, rank.astype(i32), w_lo, w_hi


def _mix_kernel(alpha,
                u_ref, uh_ref, q_ref, kc_ref, kp_ref, vc_ref, vp_ref, g_ref, x_ref,
                wdw_ref, bdw_ref, lcg_ref, lcb_ref, wco_ref, bias_ref, wao_ref,
                wo_ref, bo_ref, l1g_ref, l1b_ref, wrt_ref, br_ref,
                xa_ref, bucket_ref, rank_ref, counts_ref,
                uext, ushift, kbuf, vbuf, abuf, cbuf, tri_ref, run_ref):
    b = pl.program_id(0)
    i = pl.program_id(1)
    ts = u_ref.shape[0]

    @pl.when(jnp.logical_and(b == 0, i == 0))
    def _():
        run_ref[...] = jnp.zeros_like(run_ref)
        s = lax.broadcasted_iota(i32, (ts, ts), 0)
        t = lax.broadcasted_iota(i32, (ts, ts), 1)
        tri_ref[...] = jnp.where(s < t, 1.0, 0.0).astype(bf16)

    halo = uh_ref[...].astype(f32)
    uext[0:HALO_ROWS, :] = jnp.where(i == 0, 0.0, halo)
    uext[HALO_ROWS:HALO_ROWS + ts, :] = u_ref[...].astype(f32)
    span = ts + HALO_ROWS - SUBLANES
    for p in range(1, SUBLANES):
        ushift[p - 1, 0:span, :] = uext[p:p + span, :]
    bdw = bdw_ref[...]
    lcg = lcg_ref[...]
    lcb = lcb_ref[...]
    first_tap = HALO_ROWS - (CONV_WIDTH - 1)
    for rc in range(ts // CONV_ROWS):
        r0 = rc * CONV_ROWS
        acc = jnp.zeros((CONV_ROWS, D_CONV), f32)
        for w in range(CONV_WIDTH):
            off = first_tap + w
            p = off % SUBLANES
            base = r0 + off - p
            if p == 0:
                rows = uext[base:base + CONV_ROWS, :]
            else:
                rows = ushift[p - 1, base:base + CONV_ROWS, :]
            acc = acc + rows * wdw_ref[w:w + 1, :]
        y = _layer_norm(acc + bdw, lcg, lcb)
        y = y * _sigmoid(y)
        cbuf[r0:r0 + CONV_ROWS, :] = y.astype(bf16)

    kbuf[0:ts, :] = kp_ref[...]
    kbuf[ts:2 * ts, :] = kc_ref[...]
    vbuf[0:ts, :] = vp_ref[...]
    vbuf[ts:2 * ts, :] = vc_ref[...]
    col = lax.broadcasted_iota(i32, (2 * CHUNK, BAND), 1)
    first_head = lax.broadcasted_iota(i32, (CHUNK, 2 * HEAD_DIM), 1) < HEAD_DIM

    def chunk_body(c, carry):
        r0 = pl.multiple_of(c * CHUNK, CHUNK)
        valid = col >= (LEFT - i * ts - c * CHUNK)
        for hp in range(ATTN_HEADS // 2):
            lanes = slice(hp * 2 * HEAD_DIM, (hp + 1) * 2 * HEAD_DIM)
            q2 = q_ref[pl.ds(r0, CHUNK), lanes]
            k2 = kbuf[pl.ds(r0, BAND), lanes]
            v2 = vbuf[pl.ds(r0, BAND), lanes]
            zero = jnp.zeros_like(q2)
            qq = jnp.concatenate([jnp.where(first_head, q2, zero),
                                  jnp.where(first_head, zero, q2)], axis=0)
            s = lax.dot_general(qq, k2, (((1,), (1,)), ((), ())),
                                preferred_element_type=f32)
            s = s + bias_ref[hp]
            s = jnp.where(valid, s, NEG_BIG)
            m = jnp.max(s, axis=-1, keepdims=True)
            p = jnp.exp2(s - m)
            den = jnp.sum(p, axis=-1, keepdims=True)
            o = jnp.dot(p.astype(bf16), v2, preferred_element_type=f32) / den
            o = jnp.where(first_head, o[0:CHUNK], o[CHUNK:2 * CHUNK])
            abuf[pl.ds(r0, CHUNK), lanes] = o.astype(bf16)
        return carry

    lax.fori_loop(0, ts // CHUNK, chunk_body, 0)

    y_conv = jnp.dot(cbuf[...], wco_ref[...], preferred_element_type=f32)
    y_attn = jnp.dot(abuf[...], wao_ref[...], preferred_element_type=f32)
    merged = (g_ref[:, 0:D_MODEL].astype(f32) * y_conv
              + g_ref[:, D_MODEL:2 * D_MODEL].astype(f32) * y_attn)
    mixed = jnp.dot(merged.astype(bf16), wo_ref[...], preferred_element_type=f32) + bo_ref[...]
    x1 = _layer_norm(alpha * x_ref[...] + mixed, l1g_ref[...], l1b_ref[...])

    bucket, rank, w_lo, w_hi = _route(x1, wrt_ref, br_ref, tri_ref, run_ref)
    bucket_ref[0] = bucket
    rank_ref[0] = rank
    counts_ref[...] = run_ref[...]
    wrows = jnp.concatenate([w_lo, w_hi, jnp.zeros((LANES - 2, ts), f32)], axis=0)
    xa_ref[:, 0:D_MODEL] = x1
    xa_ref[:, D_MODEL:AUG] = wrows.T


def _mix(alpha, nb, ns, u, q, k, v, g, x, wdw, bdw, lcg, lcb, wco, bias, wao, wo, bo, l1g, l1b,
         wrt, br):
    ts = TOKEN_TILE
    nst = ns // ts
    t = nb * ns
    halo_per_tile = ts // HALO_ROWS
    cur = lambda n: pl.BlockSpec((ts, n), lambda b, i: (b * nst + i, 0))
    prev = lambda n: pl.BlockSpec((ts, n), lambda b, i: (jnp.maximum(b * nst + i - 1, 0), 0))
    const = lambda shape: pl.BlockSpec(shape, lambda b, i: (0,) * len(shape))
    meta = pl.BlockSpec((1, 1, ts), lambda b, i: (b * nst + i, 0, 0))
    in_specs = [
        cur(D_CONV),
        pl.BlockSpec((HALO_ROWS, D_CONV),
                     lambda b, i: (jnp.maximum((b * nst + i) * halo_per_tile - 1, 0), 0)),
        cur(D_ATTN), cur(D_ATTN), prev(D_ATTN), cur(D_ATTN), prev(D_ATTN),
        cur(2 * D_MODEL), cur(D_MODEL),
        const((HALO_ROWS, D_CONV)), const((1, D_CONV)), const((1, D_CONV)), const((1, D_CONV)),
        const((D_CONV, D_MODEL)), const((ATTN_HEADS // 2, 2 * CHUNK, BAND)),
        const((D_ATTN, D_MODEL)),
        const((D_MODEL, D_MODEL)), const((1, D_MODEL)), const((1, D_MODEL)), const((1, D_MODEL)),
        const((N_EXPERTS, D_MODEL)), const((N_EXPERTS, 1)),
    ]
    out_specs = [
        pl.BlockSpec((ts, AUG), lambda b, i: (b * nst + i, 0)),
        meta, meta,
        pl.BlockSpec((BUCKET_ROWS, LANES), lambda b, i: (0, 0)),
    ]
    out_shape = [
        jax.ShapeDtypeStruct((t, AUG), f32),
        jax.ShapeDtypeStruct((t // ts, 1, ts), i32),
        jax.ShapeDtypeStruct((t // ts, 1, ts), i32),
        jax.ShapeDtypeStruct((BUCKET_ROWS, LANES), f32),
    ]
    scratch = [
        pltpu.VMEM((HALO_ROWS + ts, D_CONV), f32),
        pltpu.VMEM((SUBLANES - 1, HALO_ROWS + ts, D_CONV), f32),
        pltpu.VMEM((2 * ts, D_ATTN), bf16),
        pltpu.VMEM((2 * ts, D_ATTN), bf16),
        pltpu.VMEM((ts, D_ATTN), bf16),
        pltpu.VMEM((ts, D_CONV), bf16),
        pltpu.VMEM((ts, ts), bf16),
        pltpu.VMEM((BUCKET_ROWS, LANES), f32),
    ]
    return pl.pallas_call(
        functools.partial(_mix_kernel, alpha),
        grid=(nb, nst),
        in_specs=in_specs, out_specs=out_specs, out_shape=out_shape,
        scratch_shapes=scratch,
        compiler_params=pltpu.CompilerParams(
            dimension_semantics=("arbitrary", "arbitrary"), vmem_limit_bytes=VMEM_LIMIT),
        name="mix",
    )(u, u, q, k, k, v, v, g, x, wdw, bdw, lcg, lcb, wco, bias, wao, wo, bo, l1g, l1b, wrt, br)


def _scatter_kernel(pos_ref, xa_ref, xs_in_hbm, xs_hbm, sem):
    del xs_in_hbm
    _scatter_rows("start", pos_ref, xa_ref, xs_hbm, sem)
    _scatter_rows("wait", pos_ref, xa_ref, xs_hbm, sem)


def _scatter(pos, xa, n_rows):
    nt, _, ts = pos.shape
    xs0 = jnp.zeros((n_rows, AUG), f32)
    return pl.pallas_call(
        _scatter_kernel,
        grid=(nt,),
        in_specs=[pl.BlockSpec((1, 1, ts), lambda i: (i, 0, 0), memory_space=pltpu.SMEM),
                  pl.BlockSpec((ts, AUG), lambda i: (i, 0)),
                  pl.BlockSpec(memory_space=pl.ANY)],
        out_specs=pl.BlockSpec(memory_space=pl.ANY),
        out_shape=jax.ShapeDtypeStruct((n_rows, AUG), f32),
        scratch_shapes=[pltpu.SemaphoreType.DMA(())],
        input_output_aliases={2: 0},
        compiler_params=pltpu.CompilerParams(
            dimension_semantics=("arbitrary",), has_side_effects=True,
            vmem_limit_bytes=VMEM_LIMIT),
        name="scatter",
    )(pos, xa, xs0)


def _expert_kernel(alpha, elo_ref, ehi_ref, valid_ref, xs_ref, wgl_ref, wgh_ref, wdl_ref, wdh_ref,
                   g_ref, b_ref, ys_ref):
    j = pl.program_id(0)

    @pl.when(valid_ref[j] == 1)
    def _():
        x32 = xs_ref[:, 0:D_MODEL]
        x = x32.astype(bf16)
        y = alpha * x32
        for e, (wg_ref, wd_ref) in enumerate(((wgl_ref, wdl_ref), (wgh_ref, wdh_ref))):
            gu = jnp.dot(x, wg_ref[...], preferred_element_type=f32)
            gate = gu[:, 0:D_FF_EXPERT]
            up = gu[:, D_FF_EXPERT:2 * D_FF_EXPERT]
            h = (gate * _sigmoid(gate)) * up
            d = jnp.dot(h.astype(bf16), wd_ref[...], preferred_element_type=f32)
            y = y + xs_ref[:, D_MODEL + e:D_MODEL + e + 1] * d
        ys_ref[...] = _layer_norm(y, g_ref[...], b_ref[...])

    @pl.when(valid_ref[j] == 0)
    def _():
        ys_ref[...] = jnp.zeros_like(ys_ref)


def _experts(alpha, e_lo, e_hi, valid, xs, wgu, wd, g, b):
    n_rows = xs.shape[0]
    tm = EXPERT_TILE
    grid_spec = pltpu.PrefetchScalarGridSpec(
        num_scalar_prefetch=3,
        grid=(n_rows // tm,),
        in_specs=[
            pl.BlockSpec((tm, AUG), lambda j, lo, hi, ok: (j, 0)),
            pl.BlockSpec((None, D_MODEL, 2 * D_FF_EXPERT), lambda j, lo, hi, ok: (lo[j], 0, 0)),
            pl.BlockSpec((None, D_MODEL, 2 * D_FF_EXPERT), lambda j, lo, hi, ok: (hi[j], 0, 0)),
            pl.BlockSpec((None, D_FF_EXPERT, D_MODEL), lambda j, lo, hi, ok: (lo[j], 0, 0)),
            pl.BlockSpec((None, D_FF_EXPERT, D_MODEL), lambda j, lo, hi, ok: (hi[j], 0, 0)),
            pl.BlockSpec((1, D_MODEL), lambda j, lo, hi, ok: (0, 0)),
            pl.BlockSpec((1, D_MODEL), lambda j, lo, hi, ok: (0, 0)),
        ],
        out_specs=pl.BlockSpec((tm, D_MODEL), lambda j, lo, hi, ok: (j, 0)),
    )
    return pl.pallas_call(
        functools.partial(_expert_kernel, alpha),
        grid_spec=grid_spec,
        out_shape=jax.ShapeDtypeStruct((n_rows, D_MODEL), f32),
        compiler_params=pltpu.CompilerParams(
            dimension_semantics=("arbitrary",), vmem_limit_bytes=VMEM_LIMIT),
        name="experts",
    )(e_lo, e_hi, valid, xs, wgu, wgu, wd, wd, g, b)


def _gather_kernel(pos_ref, nxt_ref, ys_hbm, out_ref, buf, sem):
    slot = _gather_step(pos_ref, nxt_ref, ys_hbm, buf, sem)
    out_ref[...] = buf[slot]


def _gather(pos, ys):
    nt, _, ts = pos.shape
    cur, nxt = _pos_specs(nt, ts)
    return pl.pallas_call(
        _gather_kernel,
        grid=(nt,),
        in_specs=[cur, nxt, pl.BlockSpec(memory_space=pl.ANY)],
        out_specs=pl.BlockSpec((ts, D_MODEL), lambda i: (i, 0)),
        out_shape=jax.ShapeDtypeStruct((nt * ts, D_MODEL), f32),
        scratch_shapes=[pltpu.VMEM((2, ts, D_MODEL), f32), pltpu.SemaphoreType.DMA((2,))],
        compiler_params=pltpu.CompilerParams(
            dimension_semantics=("arbitrary",), vmem_limit_bytes=VMEM_LIMIT),
        name="gather",
    )(pos, pos, ys)


def _bucket_tables():
    lo = np.array([g * EXPERTS_PER_GROUP + a for g in range(N_GROUPS) for a, _ in PAIRS], np.int32)
    hi = np.array([g * EXPERTS_PER_GROUP + b for g in range(N_GROUPS) for _, b in PAIRS], np.int32)
    return lo, hi


def _rel_index():
    qi = np.arange(CHUNK)[:, None]
    kj = np.arange(BAND)[None, :]
    return np.clip(qi - kj + LEFT, -MAX_REL, MAX_REL) + MAX_REL


def _table_lookup(table, idx):
    out = jnp.zeros(idx.shape, table.dtype)
    for n in range(table.shape[0]):
        out = out + jnp.where(idx == n, table[n], 0)
    return out


def kernel(x, w_in, b_in, w_dw, b_dw, ln_conv_g, ln_conv_b, w_conv_out, rel_bias, w_attn_out,
           w_o, b_o, ln1_g, ln1_b, w_router, b_router, w_gate_up, w_down, ln2_g, ln2_b):
    nb, ns, d = x.shape
    depth = w_in.shape[0]
    assert d == D_MODEL and ns % TOKEN_TILE == 0 and TOKEN_TILE >= LEFT
    alpha = (2.0 * depth) ** 0.25
    t = nb * ns
    n_tiles = t // EXPERT_TILE + N_BUCKETS
    n_rows = n_tiles * EXPERT_TILE
    lo_np, hi_np = _bucket_tables()
    lo_tab, hi_tab = jnp.asarray(lo_np), jnp.asarray(hi_np)
    rel_idx = _rel_index()
    row = lambda a: a.reshape(1, -1)

    wrt = w_router.T
    br = b_router.reshape(N_EXPERTS, 1)
    h = x.reshape(t, d)
    pos = ys = None
    for l in range(depth):
        w_in_l = w_in[l].astype(bf16)
        if l == 0:
            u, q, k, v, g = _proj(h, w_in_l, row(b_in[l]))
        else:
            h, u, q, k, v, g = _proj_gather(pos, ys, w_in_l, row(b_in[l]))
        wdw = jnp.zeros((HALO_ROWS, D_CONV), f32).at[:CONV_WIDTH].set(w_dw[l])
        bias = (rel_bias[l] * LOG2E)[:, rel_idx].reshape(ATTN_HEADS // 2, 2 * CHUNK, BAND)
        xa, bucket, rank, counts = _mix(
            alpha, nb, ns, u, q, k, v, g, h, wdw, row(b_dw[l]), row(ln_conv_g[l]),
            row(ln_conv_b[l]), w_conv_out[l].astype(bf16), bias, w_attn_out[l].astype(bf16),
            w_o[l].astype(bf16), row(b_o[l]), row(ln1_g[l]), row(ln1_b[l]), wrt, br)

        cnt = counts[:N_BUCKETS, 0].astype(i32)
        tiles = (cnt + EXPERT_TILE - 1) // EXPERT_TILE
        tile_end = jnp.cumsum(tiles)
        row_start = (tile_end - tiles) * EXPERT_TILE
        pos = _table_lookup(row_start, bucket) + rank
        tile_ids = jnp.arange(n_tiles, dtype=i32)
        tile_bucket = jnp.minimum(
            jnp.sum((tile_ids[:, None] >= tile_end[None, :]).astype(i32), axis=1), N_BUCKETS - 1)
        valid = (tile_ids < tile_end[-1]).astype(i32)

        xs = _scatter(pos, xa, n_rows)
        ys = _experts(alpha, _table_lookup(lo_tab, tile_bucket), _table_lookup(hi_tab, tile_bucket),
                      valid, xs, w_gate_up[l].astype(bf16), w_down[l].astype(bf16),
                      row(ln2_g[l]), row(ln2_b[l]))
    return _gather(pos, ys).reshape(nb, ns, d)
```

```python
import functools
import math

import jax
import jax.numpy as jnp
import numpy as np
from jax import lax
from jax.experimental import pallas as pl
from jax.experimental.pallas import tpu as pltpu

f32 = jnp.float32
bf16 = jnp.bfloat16
i32 = jnp.int32

D_MODEL = 1024
CHUNK = 64
N_LEFT_CHUNKS = 8
LEFT = N_LEFT_CHUNKS * CHUNK
BAND = LEFT + CHUNK
ATTN_HEADS = 8
HEAD_DIM = 64
D_ATTN = ATTN_HEADS * HEAD_DIM
MAX_REL = 256
D_CONV = D_MODEL - D_ATTN
CONV_WIDTH = 31
D_IN = 2 * D_CONV + 3 * D_ATTN + 2 * D_MODEL
N_EXPERTS = 16
N_GROUPS = 4
EXPERTS_PER_GROUP = N_EXPERTS // N_GROUPS
D_FF_EXPERT = 512
LN_EPS = 1e-5
LOG2E = math.log2(math.e)

PAIRS = [(a, b) for a in range(EXPERTS_PER_GROUP) for b in range(a + 1, EXPERTS_PER_GROUP)]
N_PAIRS = len(PAIRS)
N_BUCKETS = N_GROUPS * N_PAIRS
BUCKET_ROWS = 32

LANES = 128
SUBLANES = 8
TOKEN_TILE = 512
EXPERT_TILE = 256
HALO_ROWS = 32
CONV_ROWS = 64
HEAD_PAIRS = ATTN_HEADS // 2
AUG = D_MODEL + LANES
DMA_UNROLL = 8
VMEM_LIMIT = 56 * 1024 * 1024
NEG_BIG = -1e30


def _layer_norm(x, g, b):
    mu = jnp.mean(x, axis=-1, keepdims=True)
    xc = x - mu
    var = jnp.mean(xc * xc, axis=-1, keepdims=True)
    return xc * lax.rsqrt(var + LN_EPS) * g + b


def _sigmoid(x):
    return 1.0 / (1.0 + jnp.exp(-x))


def _row_copy(src, src_row, dst, dst_row, sem):
    return pltpu.make_async_copy(src.at[pl.ds(src_row, 1)], dst.at[pl.ds(dst_row, 1)], sem)


def _gather_rows(op, idx_ref, src_hbm, dst, sem):
    def body(r, c):
        getattr(_row_copy(src_hbm, idx_ref[0, 0, r], dst, r, sem), op)()
        return c
    lax.fori_loop(0, idx_ref.shape[-1], body, 0, unroll=DMA_UNROLL)


def _scatter_rows(op, idx_ref, src, dst_hbm, sem):
    def body(r, c):
        getattr(_row_copy(src, r, dst_hbm, idx_ref[0, 0, r], sem), op)()
        return c
    lax.fori_loop(0, idx_ref.shape[-1], body, 0, unroll=DMA_UNROLL)


def _gather_step(pos_ref, nxt_ref, src_hbm, buf, sem):
    i = pl.program_id(0)
    slot = lax.rem(i, 2)

    @pl.when(i == 0)
    def _():
        _gather_rows("start", pos_ref, src_hbm, buf.at[0], sem.at[0])

    @pl.when(i + 1 < pl.num_programs(0))
    def _():
        _gather_rows("start", nxt_ref, src_hbm, buf.at[1 - slot], sem.at[1 - slot])

    _gather_rows("wait", pos_ref, src_hbm, buf.at[slot], sem.at[slot])
    return slot


def _pos_specs(nt, ts):
    cur = pl.BlockSpec((1, 1, ts), lambda i: (i, 0, 0), memory_space=pltpu.SMEM)
    nxt = pl.BlockSpec((1, 1, ts), lambda i: (jnp.minimum(i + 1, nt - 1), 0, 0),
                       memory_space=pltpu.SMEM)
    return cur, nxt


def _conv_branch(u, first_of_seq, wdw_ref, bdw_ref, lcg_ref, lcb_ref, c_ref, uext, ushift):
    tm = u.shape[0]

    @pl.when(first_of_seq)
    def _():
        uext[0:HALO_ROWS, :] = jnp.zeros((HALO_ROWS, D_CONV), f32)

    @pl.when(jnp.logical_not(first_of_seq))
    def _():
        uext[0:HALO_ROWS, :] = uext[tm:tm + HALO_ROWS, :]

    uext[HALO_ROWS:HALO_ROWS + tm, :] = u
    span = tm + HALO_ROWS - SUBLANES
    for p in range(1, SUBLANES):
        ushift[p - 1, 0:span, :] = uext[p:p + span, :]
    bdw = bdw_ref[...]
    lcg = lcg_ref[...]
    lcb = lcb_ref[...]
    first_tap = HALO_ROWS - (CONV_WIDTH - 1)

    def conv_rows(rc):
        r0 = rc * CONV_ROWS
        acc = jnp.zeros((CONV_ROWS, D_CONV), f32)
        for w in range(CONV_WIDTH):
            off = first_tap + w
            p = off % SUBLANES
            base = r0 + off - p
            if p == 0:
                rows = uext[base:base + CONV_ROWS, :]
            else:
                rows = ushift[p - 1, base:base + CONV_ROWS, :]
            acc = acc + rows * wdw_ref[w:w + 1, :]
        y = _layer_norm(acc + bdw, lcg, lcb)
        y = y * _sigmoid(y)
        c_ref[r0:r0 + CONV_ROWS, :] = y.astype(bf16)

    return [functools.partial(conv_rows, rc) for rc in range(tm // CONV_ROWS)]


def _proj_body(tiles_per_seq, x, w_ref, b_ref, conv_refs, out_refs, conv_scratch):
    c_ref, q_ref, k_ref, v_ref, g_ref = out_refs
    x = x.astype(bf16)

    def seg(lo, hi):
        return jnp.dot(x, w_ref[:, lo:hi], preferred_element_type=f32) + b_ref[:, lo:hi]

    o = 0
    a = seg(o, o + D_CONV)
    gate = seg(o + D_CONV, o + 2 * D_CONV)
    first_of_seq = lax.rem(pl.program_id(0), tiles_per_seq) == 0
    conv_steps = _conv_branch(a * _sigmoid(gate), first_of_seq, *conv_refs, c_ref, *conv_scratch)
    o += 2 * D_CONV

    def q_step(o=o):
        q_ref[...] = (seg(o, o + D_ATTN) * (HEAD_DIM ** -0.5 * LOG2E)).astype(bf16)

    def copy_step(ref, o):
        ref[...] = seg(o, o + D_ATTN).astype(bf16)

    def gate_step(n, o):
        g_ref[:, n * D_ATTN:(n + 1) * D_ATTN] = _sigmoid(seg(o, o + D_ATTN)).astype(bf16)

    matmul_steps = [q_step,
                    functools.partial(copy_step, k_ref, o + D_ATTN),
                    functools.partial(copy_step, v_ref, o + 2 * D_ATTN)]
    o += 3 * D_ATTN
    matmul_steps += [functools.partial(gate_step, n, o + n * D_ATTN)
                     for n in range(2 * D_MODEL // D_ATTN)]
    for n in range(max(len(conv_steps), len(matmul_steps))):
        if n < len(conv_steps):
            conv_steps[n]()
        if n < len(matmul_steps):
            matmul_steps[n]()


def _proj_kernel(tiles_per_seq, x_ref, w_ref, b_ref, *rest):
    conv_refs, out_refs, conv_scratch = rest[:4], rest[4:9], rest[9:]
    _proj_body(tiles_per_seq, x_ref[...], w_ref, b_ref, conv_refs, out_refs, conv_scratch)


def _proj_gather_kernel(tiles_per_seq, pos_ref, nxt_ref, ys_hbm, w_ref, b_ref, *rest):
    conv_refs, x_ref, out_refs = rest[:4], rest[4], rest[5:10]
    buf, sem, conv_scratch = rest[10], rest[11], rest[12:]
    slot = _gather_step(pos_ref, nxt_ref, ys_hbm, buf, sem)
    x = buf[slot]
    x_ref[...] = x
    _proj_body(tiles_per_seq, x, w_ref, b_ref, conv_refs, out_refs, conv_scratch)


_PROJ_WIDTHS = (D_CONV, D_ATTN, D_ATTN, D_ATTN, 2 * D_MODEL)


def _proj_common(tm):
    row = lambda n: pl.BlockSpec((tm, n), lambda i: (i, 0))
    const = lambda shape: pl.BlockSpec(shape, lambda i: (0,) * len(shape))
    weight_specs = [const((D_MODEL, D_IN)), const((1, D_IN)), const((HALO_ROWS, D_CONV)),
                    const((1, D_CONV)), const((1, D_CONV)), const((1, D_CONV))]
    conv_scratch = [pltpu.VMEM((HALO_ROWS + tm, D_CONV), f32),
                    pltpu.VMEM((SUBLANES - 1, HALO_ROWS + tm, D_CONV), f32)]
    params = pltpu.CompilerParams(dimension_semantics=("arbitrary",), vmem_limit_bytes=VMEM_LIMIT)
    return row, weight_specs, conv_scratch, params


def _proj(tiles_per_seq, x, w_in, b_in, wdw, bdw, lcg, lcb):
    t = x.shape[0]
    tm = TOKEN_TILE
    row, weight_specs, conv_scratch, params = _proj_common(tm)
    return pl.pallas_call(
        functools.partial(_proj_kernel, tiles_per_seq),
        grid=(t // tm,),
        in_specs=[row(D_MODEL)] + weight_specs,
        out_specs=[row(n) for n in _PROJ_WIDTHS],
        out_shape=[jax.ShapeDtypeStruct((t, n), bf16) for n in _PROJ_WIDTHS],
        scratch_shapes=conv_scratch,
        compiler_params=params,
        name="proj",
    )(x, w_in, b_in, wdw, bdw, lcg, lcb)


def _proj_gather(tiles_per_seq, pos, ys, w_in, b_in, wdw, bdw, lcg, lcb):
    nt, _, tm = pos.shape
    t = nt * tm
    row, weight_specs, conv_scratch, params = _proj_common(tm)
    cur, nxt = _pos_specs(nt, tm)
    return pl.pallas_call(
        functools.partial(_proj_gather_kernel, tiles_per_seq),
        grid=(nt,),
        in_specs=[cur, nxt, pl.BlockSpec(memory_space=pl.ANY)] + weight_specs,
        out_specs=[row(D_MODEL)] + [row(n) for n in _PROJ_WIDTHS],
        out_shape=[jax.ShapeDtypeStruct((t, D_MODEL), f32)]
                  + [jax.ShapeDtypeStruct((t, n), bf16) for n in _PROJ_WIDTHS],
        scratch_shapes=[pltpu.VMEM((2, tm, D_MODEL), f32), pltpu.SemaphoreType.DMA((2,))]
                       + conv_scratch,
        compiler_params=params,
        name="proj_gather",
    )(pos, pos, ys, w_in, b_in, wdw, bdw, lcg, lcb)


def _route(x1, wrt_ref, br_ref, tri_ref, run_ref):
    ts = x1.shape[0]
    logits = lax.dot_general(wrt_ref[...], x1, (((1,), (1,)), ((), ())),
                             precision=lax.Precision.HIGHEST,
                             preferred_element_type=f32) + br_ref[...]
    mx = jnp.max(logits, axis=0, keepdims=True)
    ex = jnp.exp(logits - mx)
    probs = ex / jnp.sum(ex, axis=0, keepdims=True)

    best = None
    for g in range(N_GROUPS):
        r = [probs[g * EXPERTS_PER_GROUP + j:g * EXPERTS_PER_GROUP + j + 1, :]
             for j in range(EXPERTS_PER_GROUP)]
        v1 = jnp.maximum(jnp.maximum(r[0], r[1]), jnp.maximum(r[2], r[3]))
        i1 = jnp.where(r[0] == v1, 0, jnp.where(r[1] == v1, 1, jnp.where(r[2] == v1, 2, 3)))
        rm = [jnp.where(i1 == j, -1.0, r[j]) for j in range(EXPERTS_PER_GROUP)]
        v2 = jnp.maximum(jnp.maximum(rm[0], rm[1]), jnp.maximum(rm[2], rm[3]))
        i2 = jnp.where(rm[0] == v2, 0, jnp.where(rm[1] == v2, 1, jnp.where(rm[2] == v2, 2, 3)))
        cand = (v1 + v2, v1, v2, i1, i2, jnp.full_like(i1, g))
        if best is None:
            best = cand
        else:
            take = cand[0] > best[0]
            best = tuple(jnp.where(take, c, b) for c, b in zip(cand, best))
    _, v1, v2, i1, i2, grp = best
    den = v1 + v2
    w1 = v1 / den
    w2 = v2 / den
    first_is_lo = i1 < i2
    lo = jnp.where(first_is_lo, i1, i2)
    hi = jnp.where(first_is_lo, i2, i1)
    w_lo = jnp.where(first_is_lo, w1, w2)
    w_hi = jnp.where(first_is_lo, w2, w1)
    pair_base = jnp.where(lo == 0, 0, jnp.where(lo == 1, 3, 5))
    bucket = grp * N_PAIRS + pair_base + (hi - lo - 1)

    rows = lax.broadcasted_iota(i32, (BUCKET_ROWS, ts), 0)
    onehot = (rows == bucket).astype(f32)
    before = jnp.dot(onehot.astype(bf16), tri_ref[...], preferred_element_type=f32)
    rank = jnp.sum(onehot * (before + run_ref[:, 0:1]), axis=0, keepdims=True)
    run_ref[...] = run_ref[...] + jnp.sum(onehot, axis=1, keepdims=True)
    return bucket, rank.astype(i32), w_lo, w_hi


def _attention(i, q_ref, kbuf, vbuf, bias_ref, abuf, sbuf, pbuf, dbuf):
    ts = q_ref.shape[0]
    n_chunks = ts // CHUNK
    col = lax.broadcasted_iota(i32, (2 * CHUNK, BAND), 1)
    first_head = lax.broadcasted_iota(i32, (CHUNK, 2 * HEAD_DIM), 1) < HEAD_DIM
    lanes = [slice(hp * 2 * HEAD_DIM, (hp + 1) * 2 * HEAD_DIM) for hp in range(HEAD_PAIRS)]

    def scores(c):
        for hp in range(HEAD_PAIRS):
            q2 = q_ref[c * CHUNK:(c + 1) * CHUNK, lanes[hp]]
            k2 = kbuf[c * CHUNK:c * CHUNK + BAND, lanes[hp]]
            zero = jnp.zeros_like(q2)
            qq = jnp.concatenate([jnp.where(first_head, q2, zero),
                                  jnp.where(first_head, zero, q2)], axis=0)
            sbuf[c % 2, hp] = lax.dot_general(qq, k2, (((1,), (1,)), ((), ())),
                                              preferred_element_type=f32)

    def softmax(c):
        valid = col >= (LEFT - i * ts - c * CHUNK)
        for hp in range(HEAD_PAIRS):
            s = jnp.where(valid, sbuf[c % 2, hp] + bias_ref[hp], NEG_BIG)
            p = jnp.exp2(s - jnp.max(s, axis=-1, keepdims=True))
            dbuf[c % 2, hp] = jnp.sum(p, axis=-1, keepdims=True)
            pbuf[c % 2, hp] = p.astype(bf16)

    def values(c):
        for hp in range(HEAD_PAIRS):
            v2 = vbuf[c * CHUNK:c * CHUNK + BAND, lanes[hp]]
            o = jnp.dot(pbuf[c % 2, hp], v2, preferred_element_type=f32) / dbuf[c % 2, hp]
            o = jnp.where(first_head, o[0:CHUNK], o[CHUNK:2 * CHUNK])
            abuf[c * CHUNK:(c + 1) * CHUNK, lanes[hp]] = o.astype(bf16)

    scores(0)
    for c in range(n_chunks):
        if c + 1 < n_chunks:
            scores(c + 1)
        softmax(c)
        if c >= 1:
            values(c - 1)
    values(n_chunks - 1)


def _mix_kernel(alpha,
                c_ref, q_ref, kc_ref, kp_ref, vc_ref, vp_ref, g_ref, x_ref,
                wco_ref, bias_ref, wao_ref, wo_ref, bo_ref, l1g_ref, l1b_ref, wrt_ref, br_ref,
                xa_ref, bucket_ref, rank_ref, counts_ref,
                kbuf, vbuf, abuf, sbuf, pbuf, dbuf, tri_ref, run_ref):
    b = pl.program_id(0)
    i = pl.program_id(1)
    ts = q_ref.shape[0]

    @pl.when(jnp.logical_and(b == 0, i == 0))
    def _():
        run_ref[...] = jnp.zeros_like(run_ref)
        s = lax.broadcasted_iota(i32, (ts, ts), 0)
        t = lax.broadcasted_iota(i32, (ts, ts), 1)
        tri_ref[...] = jnp.where(s < t, 1.0, 0.0).astype(bf16)

    kbuf[0:ts, :] = kp_ref[...]
    kbuf[ts:2 * ts, :] = kc_ref[...]
    vbuf[0:ts, :] = vp_ref[...]
    vbuf[ts:2 * ts, :] = vc_ref[...]
    _attention(i, q_ref, kbuf, vbuf, bias_ref, abuf, sbuf, pbuf, dbuf)

    y_conv = jnp.dot(c_ref[...], wco_ref[...], preferred_element_type=f32)
    y_attn = jnp.dot(abuf[...], wao_ref[...], preferred_element_type=f32)
    merged = (g_ref[:, 0:D_MODEL].astype(f32) * y_conv
              + g_ref[:, D_MODEL:2 * D_MODEL].astype(f32) * y_attn)
    mixed = jnp.dot(merged.astype(bf16), wo_ref[...], preferred_element_type=f32) + bo_ref[...]
    x1 = _layer_norm(alpha * x_ref[...] + mixed, l1g_ref[...], l1b_ref[...])

    bucket, rank, w_lo, w_hi = _route(x1, wrt_ref, br_ref, tri_ref, run_ref)
    bucket_ref[0] = bucket
    rank_ref[0] = rank
    counts_ref[...] = run_ref[...]
    wrows = jnp.concatenate([w_lo, w_hi, jnp.zeros((LANES - 2, ts), f32)], axis=0)
    xa_ref[:, 0:D_MODEL] = x1
    xa_ref[:, D_MODEL:AUG] = wrows.T


def _mix(alpha, nb, ns, cact, q, k, v, g, x, wco, bias, wao, wo, bo, l1g, l1b, wrt, br):
    ts = TOKEN_TILE
    nst = ns // ts
    t = nb * ns
    cur = lambda n: pl.BlockSpec((ts, n), lambda b, i: (b * nst + i, 0))
    prev = lambda n: pl.BlockSpec((ts, n), lambda b, i: (jnp.maximum(b * nst + i - 1, 0), 0))
    const = lambda shape: pl.BlockSpec(shape, lambda b, i: (0,) * len(shape))
    meta = pl.BlockSpec((1, 1, ts), lambda b, i: (b * nst + i, 0, 0))
    in_specs = [
        cur(D_CONV), cur(D_ATTN), cur(D_ATTN), prev(D_ATTN), cur(D_ATTN), prev(D_ATTN),
        cur(2 * D_MODEL), cur(D_MODEL),
        const((D_CONV, D_MODEL)), const((HEAD_PAIRS, 2 * CHUNK, BAND)), const((D_ATTN, D_MODEL)),
        const((D_MODEL, D_MODEL)), const((1, D_MODEL)), const((1, D_MODEL)), const((1, D_MODEL)),
        const((N_EXPERTS, D_MODEL)), const((N_EXPERTS, 1)),
    ]
    out_specs = [
        pl.BlockSpec((ts, AUG), lambda b, i: (b * nst + i, 0)),
        meta, meta,
        pl.BlockSpec((BUCKET_ROWS, LANES), lambda b, i: (0, 0)),
    ]
    out_shape = [
        jax.ShapeDtypeStruct((t, AUG), f32),
        jax.ShapeDtypeStruct((t // ts, 1, ts), i32),
        jax.ShapeDtypeStruct((t // ts, 1, ts), i32),
        jax.ShapeDtypeStruct((BUCKET_ROWS, LANES), f32),
    ]
    scratch = [
        pltpu.VMEM((2 * ts, D_ATTN), bf16),
        pltpu.VMEM((2 * ts, D_ATTN), bf16),
        pltpu.VMEM((ts, D_ATTN), bf16),
        pltpu.VMEM((2, HEAD_PAIRS, 2 * CHUNK, BAND), f32),
        pltpu.VMEM((2, HEAD_PAIRS, 2 * CHUNK, BAND), bf16),
        pltpu.VMEM((2, HEAD_PAIRS, 2 * CHUNK, 1), f32),
        pltpu.VMEM((ts, ts), bf16),
        pltpu.VMEM((BUCKET_ROWS, LANES), f32),
    ]
    return pl.pallas_call(
        functools.partial(_mix_kernel, alpha),
        grid=(nb, nst),
        in_specs=in_specs, out_specs=out_specs, out_shape=out_shape,
        scratch_shapes=scratch,
        compiler_params=pltpu.CompilerParams(
            dimension_semantics=("arbitrary", "arbitrary"), vmem_limit_bytes=VMEM_LIMIT),
        name="mix",
    )(cact, q, k, k, v, v, g, x, wco, bias, wao, wo, bo, l1g, l1b, wrt, br)


def _scatter_kernel(pad_start_ref, pad_len_ref, used_ref, pos_ref, xa_ref, xs_hbm, zbuf, sem):
    @pl.when(pl.program_id(0) == 0)
    def _():
        zbuf[...] = jnp.zeros_like(zbuf)
        tm = zbuf.shape[0]
        n_tiles = xs_hbm.shape[0] // tm
        for op in ("start", "wait"):
            def bucket_body(b, c, op=op):
                def row_body(r, c2):
                    getattr(_row_copy(zbuf, 0, xs_hbm, pad_start_ref[b] + r, sem), op)()
                    return c2
                lax.fori_loop(0, pad_len_ref[b], row_body, 0)
                return c
            lax.fori_loop(0, N_BUCKETS, bucket_body, 0)

            def tile_body(j, c, op=op):
                dst = xs_hbm.at[pl.ds(pl.multiple_of(j * tm, tm), tm)]
                getattr(pltpu.make_async_copy(zbuf, dst, sem), op)()
                return c
            lax.fori_loop(used_ref[0], n_tiles, tile_body, 0)

    _scatter_rows("start", pos_ref, xa_ref, xs_hbm, sem)
    _scatter_rows("wait", pos_ref, xa_ref, xs_hbm, sem)


def _scatter(pad_start, pad_len, used_tiles, pos, xa, n_rows):
    nt, _, ts = pos.shape
    grid_spec = pltpu.PrefetchScalarGridSpec(
        num_scalar_prefetch=3,
        grid=(nt,),
        in_specs=[pl.BlockSpec((1, 1, ts), lambda i, *_: (i, 0, 0), memory_space=pltpu.SMEM),
                  pl.BlockSpec((ts, AUG), lambda i, *_: (i, 0))],
        out_specs=pl.BlockSpec(memory_space=pl.ANY),
        scratch_shapes=[pltpu.VMEM((EXPERT_TILE, AUG), f32), pltpu.SemaphoreType.DMA(())],
    )
    return pl.pallas_call(
        _scatter_kernel,
        grid_spec=grid_spec,
        out_shape=jax.ShapeDtypeStruct((n_rows, AUG), f32),
        compiler_params=pltpu.CompilerParams(
            dimension_semantics=("arbitrary",), has_side_effects=True,
            vmem_limit_bytes=VMEM_LIMIT),
        name="scatter",
    )(pad_start, pad_len, used_tiles, pos, xa)


def _expert_kernel(alpha, elo_ref, ehi_ref, valid_ref, xs_ref, wgl_ref, wgh_ref, wdl_ref,
                   wdh_ref, g_ref, b_ref, ys_ref):
    j = pl.program_id(0)

    @pl.when(valid_ref[j] == 1)
    def _():
        x32 = xs_ref[:, 0:D_MODEL]
        x = x32.astype(bf16)
        y = alpha * x32
        for e, (wg_ref, wd_ref) in enumerate(((wgl_ref, wdl_ref), (wgh_ref, wdh_ref))):
            gu = jnp.dot(x, wg_ref[...], preferred_element_type=f32)
            gate = gu[:, 0:D_FF_EXPERT]
            up = gu[:, D_FF_EXPERT:2 * D_FF_EXPERT]
            h = (gate * _sigmoid(gate)) * up
            d = jnp.dot(h.astype(bf16), wd_ref[...], preferred_element_type=f32)
            y = y + xs_ref[:, D_MODEL + e:D_MODEL + e + 1] * d
        ys_ref[...] = _layer_norm(y, g_ref[...], b_ref[...])

    @pl.when(valid_ref[j] == 0)
    def _():
        ys_ref[...] = jnp.zeros_like(ys_ref)


def _experts(alpha, e_lo, e_hi, valid, xs, wgu, wd, g, b):
    n_rows = xs.shape[0]
    tm = EXPERT_TILE
    grid_spec = pltpu.PrefetchScalarGridSpec(
        num_scalar_prefetch=3,
        grid=(n_rows // tm,),
        in_specs=[
            pl.BlockSpec((tm, AUG), lambda j, lo, hi, ok: (j, 0)),
            pl.BlockSpec((None, D_MODEL, 2 * D_FF_EXPERT), lambda j, lo, hi, ok: (lo[j], 0, 0)),
            pl.BlockSpec((None, D_MODEL, 2 * D_FF_EXPERT), lambda j, lo, hi, ok: (hi[j], 0, 0)),
            pl.BlockSpec((None, D_FF_EXPERT, D_MODEL), lambda j, lo, hi, ok: (lo[j], 0, 0)),
            pl.BlockSpec((None, D_FF_EXPERT, D_MODEL), lambda j, lo, hi, ok: (hi[j], 0, 0)),
            pl.BlockSpec((1, D_MODEL), lambda j, lo, hi, ok: (0, 0)),
            pl.BlockSpec((1, D_MODEL), lambda j, lo, hi, ok: (0, 0)),
        ],
        out_specs=pl.BlockSpec((tm, D_MODEL), lambda j, lo, hi, ok: (j, 0)),
    )
    return pl.pallas_call(
        functools.partial(_expert_kernel, alpha),
        grid_spec=grid_spec,
        out_shape=jax.ShapeDtypeStruct((n_rows, D_MODEL), f32),
        compiler_params=pltpu.CompilerParams(
            dimension_semantics=("arbitrary",), vmem_limit_bytes=VMEM_LIMIT),
        name="experts",
    )(e_lo, e_hi, valid, xs, wgu, wgu, wd, wd, g, b)


def _gather_kernel(pos_ref, nxt_ref, ys_hbm, out_ref, buf, sem):
    slot = _gather_step(pos_ref, nxt_ref, ys_hbm, buf, sem)
    out_ref[...] = buf[slot]


def _gather(pos, ys):
    nt, _, ts = pos.shape
    cur, nxt = _pos_specs(nt, ts)
    return pl.pallas_call(
        _gather_kernel,
        grid=(nt,),
        in_specs=[cur, nxt, pl.BlockSpec(memory_space=pl.ANY)],
        out_specs=pl.BlockSpec((ts, D_MODEL), lambda i: (i, 0)),
        out_shape=jax.ShapeDtypeStruct((nt * ts, D_MODEL), f32),
        scratch_shapes=[pltpu.VMEM((2, ts, D_MODEL), f32), pltpu.SemaphoreType.DMA((2,))],
        compiler_params=pltpu.CompilerParams(
            dimension_semantics=("arbitrary",), vmem_limit_bytes=VMEM_LIMIT),
        name="gather",
    )(pos, pos, ys)


def _bucket_tables():
    lo = np.array([g * EXPERTS_PER_GROUP + a for g in range(N_GROUPS) for a, _ in PAIRS], np.int32)
    hi = np.array([g * EXPERTS_PER_GROUP + b for g in range(N_GROUPS) for _, b in PAIRS], np.int32)
    return lo, hi


def _score_bias(rel_bias):
    diag = np.arange(CHUNK - 1 + BAND)
    idx = np.clip(LEFT + CHUNK - 1 - diag, -MAX_REL, MAX_REL) + MAX_REL
    ext = (rel_bias * LOG2E)[:, idx]
    rows = [ext[:, CHUNK - 1 - i:CHUNK - 1 - i + BAND] for i in range(CHUNK)]
    return jnp.stack(rows, axis=1).reshape(HEAD_PAIRS, 2 * CHUNK, BAND)


def _table_lookup(table, idx):
    out = jnp.zeros(idx.shape, table.dtype)
    for n in range(table.shape[0]):
        out = out + jnp.where(idx == n, table[n], 0)
    return out


def kernel(x, w_in, b_in, w_dw, b_dw, ln_conv_g, ln_conv_b, w_conv_out, rel_bias, w_attn_out,
           w_o, b_o, ln1_g, ln1_b, w_router, b_router, w_gate_up, w_down, ln2_g, ln2_b):
    nb, ns, d = x.shape
    depth = w_in.shape[0]
    assert d == D_MODEL and ns % TOKEN_TILE == 0 and TOKEN_TILE >= LEFT
    alpha = (2.0 * depth) ** 0.25
    t = nb * ns
    tiles_per_seq = ns // TOKEN_TILE
    n_tiles = t // EXPERT_TILE + N_BUCKETS
    n_rows = n_tiles * EXPERT_TILE
    lo_np, hi_np = _bucket_tables()
    lo_tab, hi_tab = jnp.asarray(lo_np), jnp.asarray(hi_np)
    row = lambda a: a.reshape(1, -1)

    wrt = w_router.T
    br = b_router.reshape(N_EXPERTS, 1)
    h = x.reshape(t, d)
    pos = ys = None
    for l in range(depth):
        wdw = jnp.zeros((HALO_ROWS, D_CONV), f32).at[:CONV_WIDTH].set(w_dw[l])
        proj_weights = (w_in[l].astype(bf16), row(b_in[l]), wdw, row(b_dw[l]),
                        row(ln_conv_g[l]), row(ln_conv_b[l]))
        if l == 0:
            cact, q, k, v, g = _proj(tiles_per_seq, h, *proj_weights)
        else:
            h, cact, q, k, v, g = _proj_gather(tiles_per_seq, pos, ys, *proj_weights)
        xa, bucket, rank, counts = _mix(
            alpha, nb, ns, cact, q, k, v, g, h, w_conv_out[l].astype(bf16),
            _score_bias(rel_bias[l]), w_attn_out[l].astype(bf16), w_o[l].astype(bf16),
            row(b_o[l]), row(ln1_g[l]), row(ln1_b[l]), wrt, br)

        cnt = counts[:N_BUCKETS, 0].astype(i32)
        tiles = (cnt + EXPERT_TILE - 1) // EXPERT_TILE
        tile_end = jnp.cumsum(tiles)
        row_start = (tile_end - tiles) * EXPERT_TILE
        pos = _table_lookup(row_start, bucket) + rank
        tile_ids = jnp.arange(n_tiles, dtype=i32)
        tile_bucket = jnp.minimum(
            jnp.sum((tile_ids[:, None] >= tile_end[None, :]).astype(i32), axis=1), N_BUCKETS - 1)
        valid = (tile_ids < tile_end[-1]).astype(i32)

        xs = _scatter(row_start + cnt, tiles * EXPERT_TILE - cnt, tile_end[-1:], pos, xa, n_rows)
        ys = _experts(alpha, _table_lookup(lo_tab, tile_bucket), _table_lookup(hi_tab, tile_bucket),
                      valid, xs, w_gate_up[l].astype(bf16), w_down[l].astype(bf16),
                      row(ln2_g[l]), row(ln2_b[l]))
    return _gather(pos, ys).reshape(nb, ns, d)
```

```python
import functools
import math

import jax
import jax.numpy as jnp
import numpy as np
from jax import lax
from jax.experimental import pallas as pl
from jax.experimental.pallas import tpu as pltpu

f32 = jnp.float32
bf16 = jnp.bfloat16
i32 = jnp.int32

D_MODEL = 1024
CHUNK = 64
N_LEFT_CHUNKS = 8
LEFT = N_LEFT_CHUNKS * CHUNK
BAND = LEFT + CHUNK
ATTN_HEADS = 8
HEAD_DIM = 64
D_ATTN = ATTN_HEADS * HEAD_DIM
MAX_REL = 256
D_CONV = D_MODEL - D_ATTN
CONV_WIDTH = 31
D_IN = 2 * D_CONV + 3 * D_ATTN + 2 * D_MODEL
N_EXPERTS = 16
N_GROUPS = 4
EXPERTS_PER_GROUP = N_EXPERTS // N_GROUPS
D_FF_EXPERT = 512
LN_EPS = 1e-5
LOG2E = math.log2(math.e)

PAIRS = [(a, b) for a in range(EXPERTS_PER_GROUP) for b in range(a + 1, EXPERTS_PER_GROUP)]
N_PAIRS = len(PAIRS)
N_BUCKETS = N_GROUPS * N_PAIRS
BUCKET_ROWS = 32

LANES = 128
SUBLANES = 8
TOKEN_TILE = 512
EXPERT_TILE = 256
HALO_ROWS = 32
CONV_ROWS = 64
HEAD_PAIRS = ATTN_HEADS // 2
AUG = D_MODEL + LANES
DMA_UNROLL = 8
GATHER_GROUPS = 8
VMEM_LIMIT = 56 * 1024 * 1024
NEG_BIG = -1e30


def _layer_norm(x, g, b):
    mu = jnp.mean(x, axis=-1, keepdims=True)
    xc = x - mu
    var = jnp.mean(xc * xc, axis=-1, keepdims=True)
    return xc * lax.rsqrt(var + LN_EPS) * g + b


def _sigmoid(x):
    return 1.0 / (1.0 + jnp.exp(-x))


def _row_copy(src, src_row, dst, dst_row, sem):
    return pltpu.make_async_copy(src.at[pl.ds(src_row, 1)], dst.at[pl.ds(dst_row, 1)], sem)


def _for_rows(n, body, op):
    if op == "start":
        for r in range(n):
            body(r)
    else:
        def step(r, c):
            body(r)
            return c
        lax.fori_loop(0, n, step, 0, unroll=DMA_UNROLL)


def _gather_rows(op, idx_ref, src_hbm, dst, sem):
    def body(r):
        getattr(_row_copy(src_hbm, idx_ref[0, 0, r], dst, r, sem), op)()
    _for_rows(idx_ref.shape[-1], body, op)


def _scatter_rows(op, idx_ref, src, dst_hbm, sem):
    def body(r):
        getattr(_row_copy(src, r, dst_hbm, idx_ref[0, 0, r], sem), op)()
    _for_rows(idx_ref.shape[-1], body, op)


def _gather_step(pos_ref, nxt_ref, src_hbm, buf, sem):
    i = pl.program_id(0)
    slot = lax.rem(i, 2)

    @pl.when(i == 0)
    def _():
        _gather_rows("start", pos_ref, src_hbm, buf.at[0], sem.at[0])

    @pl.when(i + 1 < pl.num_programs(0))
    def _():
        _gather_rows("start", nxt_ref, src_hbm, buf.at[1 - slot], sem.at[1 - slot])

    _gather_rows("wait", pos_ref, src_hbm, buf.at[slot], sem.at[slot])
    return slot


def _pos_specs(nt, ts):
    cur = pl.BlockSpec((1, 1, ts), lambda i: (i, 0, 0), memory_space=pltpu.SMEM)
    nxt = pl.BlockSpec((1, 1, ts), lambda i: (jnp.minimum(i + 1, nt - 1), 0, 0),
                       memory_space=pltpu.SMEM)
    return cur, nxt


def _conv_branch(u, first_of_seq, wdw_ref, bdw_ref, lcg_ref, lcb_ref, c_ref, uext, ushift):
    tm = u.shape[0]

    @pl.when(first_of_seq)
    def _():
        uext[0:HALO_ROWS, :] = jnp.zeros((HALO_ROWS, D_CONV), f32)

    @pl.when(jnp.logical_not(first_of_seq))
    def _():
        uext[0:HALO_ROWS, :] = uext[tm:tm + HALO_ROWS, :]

    uext[HALO_ROWS:HALO_ROWS + tm, :] = u
    span = tm + HALO_ROWS - SUBLANES
    for p in range(1, SUBLANES):
        ushift[p - 1, 0:span, :] = uext[p:p + span, :]
    bdw = bdw_ref[...]
    lcg = lcg_ref[...]
    lcb = lcb_ref[...]
    first_tap = HALO_ROWS - (CONV_WIDTH - 1)

    def conv_rows(rc):
        r0 = rc * CONV_ROWS
        acc = jnp.zeros((CONV_ROWS, D_CONV), f32)
        for w in range(CONV_WIDTH):
            off = first_tap + w
            p = off % SUBLANES
            base = r0 + off - p
            if p == 0:
                rows = uext[base:base + CONV_ROWS, :]
            else:
                rows = ushift[p - 1, base:base + CONV_ROWS, :]
            acc = acc + rows * wdw_ref[w:w + 1, :]
        y = _layer_norm(acc + bdw, lcg, lcb)
        y = y * _sigmoid(y)
        c_ref[r0:r0 + CONV_ROWS, :] = y.astype(bf16)

    return [functools.partial(conv_rows, rc) for rc in range(tm // CONV_ROWS)]


def _proj_body(tiles_per_seq, x, extra_steps, w_ref, b_ref, conv_refs, out_refs, conv_scratch):
    c_ref, q_ref, k_ref, v_ref, g_ref = out_refs
    x = x.astype(bf16)

    def seg(lo, hi):
        return jnp.dot(x, w_ref[:, lo:hi], preferred_element_type=f32) + b_ref[:, lo:hi]

    o = 0
    a = seg(o, o + D_CONV)
    gate = seg(o + D_CONV, o + 2 * D_CONV)
    first_of_seq = lax.rem(pl.program_id(0), tiles_per_seq) == 0
    conv_steps = _conv_branch(a * _sigmoid(gate), first_of_seq, *conv_refs, c_ref, *conv_scratch)
    o += 2 * D_CONV

    def q_step(o=o):
        q_ref[...] = (seg(o, o + D_ATTN) * (HEAD_DIM ** -0.5 * LOG2E)).astype(bf16)

    def copy_step(ref, o):
        ref[...] = seg(o, o + D_ATTN).astype(bf16)

    def gate_step(n, o):
        g_ref[:, n * D_ATTN:(n + 1) * D_ATTN] = _sigmoid(seg(o, o + D_ATTN)).astype(bf16)

    matmul_steps = [q_step,
                    functools.partial(copy_step, k_ref, o + D_ATTN),
                    functools.partial(copy_step, v_ref, o + 2 * D_ATTN)]
    o += 3 * D_ATTN
    matmul_steps += [functools.partial(gate_step, n, o + n * D_ATTN)
                     for n in range(2 * D_MODEL // D_ATTN)]
    for n in range(max(len(conv_steps), len(matmul_steps), len(extra_steps))):
        for steps in (conv_steps, extra_steps, matmul_steps):
            if n < len(steps):
                steps[n]()


def _proj_kernel(tiles_per_seq, x_ref, w_ref, b_ref, *rest):
    conv_refs, out_refs, conv_scratch = rest[:4], rest[4:9], rest[9:]
    _proj_body(tiles_per_seq, x_ref[...], (), w_ref, b_ref, conv_refs, out_refs, conv_scratch)


def _proj_gather_kernel(tiles_per_seq, pos_ref, nxt_ref, ys_hbm, w_ref, b_ref, *rest):
    conv_refs, x_ref, out_refs = rest[:4], rest[4], rest[5:10]
    buf, sem, conv_scratch = rest[10], rest[11], rest[12:]
    i = pl.program_id(0)
    tm = x_ref.shape[0]
    slot = lax.rem(i, 2)

    @pl.when(i == 0)
    def _():
        _gather_rows("start", pos_ref, ys_hbm, buf.at[0], sem.at[0])

    _gather_rows("wait", pos_ref, ys_hbm, buf.at[slot], sem.at[slot])

    def start_rows(lo):
        for r in range(lo, lo + tm // GATHER_GROUPS):
            _row_copy(ys_hbm, nxt_ref[0, 0, r], buf.at[1 - slot], r, sem.at[1 - slot]).start()

    start_steps = [functools.partial(start_rows, n * (tm // GATHER_GROUPS))
                   for n in range(GATHER_GROUPS)]
    x = buf[slot]
    x_ref[...] = x
    _proj_body(tiles_per_seq, x, start_steps, w_ref, b_ref, conv_refs, out_refs, conv_scratch)

    @pl.when(i == pl.num_programs(0) - 1)
    def _():
        _gather_rows("wait", nxt_ref, ys_hbm, buf.at[1 - slot], sem.at[1 - slot])


_PROJ_WIDTHS = (D_CONV, D_ATTN, D_ATTN, D_ATTN, 2 * D_MODEL)


def _proj_common(tm):
    row = lambda n: pl.BlockSpec((tm, n), lambda i: (i, 0))
    const = lambda shape: pl.BlockSpec(shape, lambda i: (0,) * len(shape))
    weight_specs = [const((D_MODEL, D_IN)), const((1, D_IN)), const((HALO_ROWS, D_CONV)),
                    const((1, D_CONV)), const((1, D_CONV)), const((1, D_CONV))]
    conv_scratch = [pltpu.VMEM((HALO_ROWS + tm, D_CONV), f32),
                    pltpu.VMEM((SUBLANES - 1, HALO_ROWS + tm, D_CONV), f32)]
    params = pltpu.CompilerParams(dimension_semantics=("arbitrary",), vmem_limit_bytes=VMEM_LIMIT)
    return row, weight_specs, conv_scratch, params


def _proj(tiles_per_seq, x, w_in, b_in, wdw, bdw, lcg, lcb):
    t = x.shape[0]
    tm = TOKEN_TILE
    row, weight_specs, conv_scratch, params = _proj_common(tm)
    return pl.pallas_call(
        functools.partial(_proj_kernel, tiles_per_seq),
        grid=(t // tm,),
        in_specs=[row(D_MODEL)] + weight_specs,
        out_specs=[row(n) for n in _PROJ_WIDTHS],
        out_shape=[jax.ShapeDtypeStruct((t, n), bf16) for n in _PROJ_WIDTHS],
        scratch_shapes=conv_scratch,
        compiler_params=params,
        name="proj",
    )(x, w_in, b_in, wdw, bdw, lcg, lcb)


def _proj_gather(tiles_per_seq, pos, ys, w_in, b_in, wdw, bdw, lcg, lcb):
    nt, _, tm = pos.shape
    t = nt * tm
    row, weight_specs, conv_scratch, params = _proj_common(tm)
    cur, nxt = _pos_specs(nt, tm)
    return pl.pallas_call(
        functools.partial(_proj_gather_kernel, tiles_per_seq),
        grid=(nt,),
        in_specs=[cur, nxt, pl.BlockSpec(memory_space=pl.ANY)] + weight_specs,
        out_specs=[row(D_MODEL)] + [row(n) for n in _PROJ_WIDTHS],
        out_shape=[jax.ShapeDtypeStruct((t, D_MODEL), f32)]
                  + [jax.ShapeDtypeStruct((t, n), bf16) for n in _PROJ_WIDTHS],
        scratch_shapes=[pltpu.VMEM((2, tm, D_MODEL), f32), pltpu.SemaphoreType.DMA((2,))]
                       + conv_scratch,
        compiler_params=params,
        name="proj_gather",
    )(pos, pos, ys, w_in, b_in, wdw, bdw, lcg, lcb)


def _route(x1, wrt_ref, br_ref, tri_ref, run_ref):
    ts = x1.shape[0]
    logits = lax.dot_general(wrt_ref[...], x1, (((1,), (1,)), ((), ())),
                             precision=lax.Precision.HIGHEST,
                             preferred_element_type=f32) + br_ref[...]
    mx = jnp.max(logits, axis=0, keepdims=True)
    ex = jnp.exp(logits - mx)
    probs = ex / jnp.sum(ex, axis=0, keepdims=True)

    best = None
    for g in range(N_GROUPS):
        r = [probs[g * EXPERTS_PER_GROUP + j:g * EXPERTS_PER_GROUP + j + 1, :]
             for j in range(EXPERTS_PER_GROUP)]
        v1 = jnp.maximum(jnp.maximum(r[0], r[1]), jnp.maximum(r[2], r[3]))
        i1 = jnp.where(r[0] == v1, 0, jnp.where(r[1] == v1, 1, jnp.where(r[2] == v1, 2, 3)))
        rm = [jnp.where(i1 == j, -1.0, r[j]) for j in range(EXPERTS_PER_GROUP)]
        v2 = jnp.maximum(jnp.maximum(rm[0], rm[1]), jnp.maximum(rm[2], rm[3]))
        i2 = jnp.where(rm[0] == v2, 0, jnp.where(rm[1] == v2, 1, jnp.where(rm[2] == v2, 2, 3)))
        cand = (v1 + v2, v1, v2, i1, i2, jnp.full_like(i1, g))
        if best is None:
            best = cand
        else:
            take = cand[0] > best[0]
            best = tuple(jnp.where(take, c, b) for c, b in zip(cand, best))
    _, v1, v2, i1, i2, grp = best
    den = v1 + v2
    w1 = v1 / den
    w2 = v2 / den
    first_is_lo = i1 < i2
    lo = jnp.where(first_is_lo, i1, i2)
    hi = jnp.where(first_is_lo, i2, i1)
    w_lo = jnp.where(first_is_lo, w1, w2)
    w_hi = jnp.where(first_is_lo, w2, w1)
    pair_base = jnp.where(lo == 0, 0, jnp.where(lo == 1, 3, 5))
    bucket = grp * N_PAIRS + pair_base + (hi - lo - 1)

    rows = lax.broadcasted_iota(i32, (BUCKET_ROWS, ts), 0)
    onehot = (rows == bucket).astype(f32)
    before = jnp.dot(onehot.astype(bf16), tri_ref[...], preferred_element_type=f32)
    rank = jnp.sum(onehot * (before + run_ref[:, 0:1]), axis=0, keepdims=True)
    run_ref[...] = run_ref[...] + jnp.sum(onehot, axis=1, keepdims=True)
    return bucket, rank.astype(i32), w_lo, w_hi


def _attention(i, q_ref, kbuf, vbuf, bias_ref, abuf, sbuf, pbuf, dbuf):
    ts = q_ref.shape[0]
    n_chunks = ts // CHUNK
    col = lax.broadcasted_iota(i32, (2 * CHUNK, BAND), 1)
    first_head = lax.broadcasted_iota(i32, (CHUNK, 2 * HEAD_DIM), 1) < HEAD_DIM
    lanes = [slice(hp * 2 * HEAD_DIM, (hp + 1) * 2 * HEAD_DIM) for hp in range(HEAD_PAIRS)]

    def scores(c):
        for hp in range(HEAD_PAIRS):
            q2 = q_ref[c * CHUNK:(c + 1) * CHUNK, lanes[hp]]
            k2 = kbuf[c * CHUNK:c * CHUNK + BAND, lanes[hp]]
            zero = jnp.zeros_like(q2)
            qq = jnp.concatenate([jnp.where(first_head, q2, zero),
                                  jnp.where(first_head, zero, q2)], axis=0)
            sbuf[c % 2, hp] = lax.dot_general(qq, k2, (((1,), (1,)), ((), ())),
                                              preferred_element_type=f32)

    def softmax(c):
        valid = col >= (LEFT - i * ts - c * CHUNK)
        for hp in range(HEAD_PAIRS):
            s = jnp.where(valid, sbuf[c % 2, hp] + bias_ref[hp], NEG_BIG)
            p = jnp.exp2(s - jnp.max(s, axis=-1, keepdims=True))
            dbuf[c % 2, hp] = jnp.sum(p, axis=-1, keepdims=True)
            pbuf[c % 2, hp] = p.astype(bf16)

    def values(c):
        for hp in range(HEAD_PAIRS):
            v2 = vbuf[c * CHUNK:c * CHUNK + BAND, lanes[hp]]
            o = jnp.dot(pbuf[c % 2, hp], v2, preferred_element_type=f32) / dbuf[c % 2, hp]
            o = jnp.where(first_head, o[0:CHUNK], o[CHUNK:2 * CHUNK])
            abuf[c * CHUNK:(c + 1) * CHUNK, lanes[hp]] = o.astype(bf16)

    scores(0)
    for c in range(n_chunks):
        if c + 1 < n_chunks:
            scores(c + 1)
        softmax(c)
        if c >= 1:
            values(c - 1)
    values(n_chunks - 1)


def _mix_kernel(alpha,
                c_ref, q_ref, kc_ref, kp_ref, vc_ref, vp_ref, g_ref, x_ref,
                wco_ref, bias_ref, wao_ref, wo_ref, bo_ref, l1g_ref, l1b_ref, wrt_ref, br_ref,
                xa_ref, bucket_ref, rank_ref, counts_ref,
                kbuf, vbuf, abuf, sbuf, pbuf, dbuf, tri_ref, run_ref):
    b = pl.program_id(0)
    i = pl.program_id(1)
    ts = q_ref.shape[0]

    @pl.when(jnp.logical_and(b == 0, i == 0))
    def _():
        run_ref[...] = jnp.zeros_like(run_ref)
        s = lax.broadcasted_iota(i32, (ts, ts), 0)
        t = lax.broadcasted_iota(i32, (ts, ts), 1)
        tri_ref[...] = jnp.where(s < t, 1.0, 0.0).astype(bf16)

    kbuf[0:ts, :] = kp_ref[...]
    kbuf[ts:2 * ts, :] = kc_ref[...]
    vbuf[0:ts, :] = vp_ref[...]
    vbuf[ts:2 * ts, :] = vc_ref[...]
    _attention(i, q_ref, kbuf, vbuf, bias_ref, abuf, sbuf, pbuf, dbuf)

    y_conv = jnp.dot(c_ref[...], wco_ref[...], preferred_element_type=f32)
    y_attn = jnp.dot(abuf[...], wao_ref[...], preferred_element_type=f32)
    merged = (g_ref[:, 0:D_MODEL].astype(f32) * y_conv
              + g_ref[:, D_MODEL:2 * D_MODEL].astype(f32) * y_attn)
    mixed = jnp.dot(merged.astype(bf16), wo_ref[...], preferred_element_type=f32) + bo_ref[...]
    x1 = _layer_norm(alpha * x_ref[...] + mixed, l1g_ref[...], l1b_ref[...])

    bucket, rank, w_lo, w_hi = _route(x1, wrt_ref, br_ref, tri_ref, run_ref)
    bucket_ref[0] = bucket
    rank_ref[0] = rank
    counts_ref[...] = run_ref[...]
    wrows = jnp.concatenate([w_lo, w_hi, jnp.zeros((LANES - 2, ts), f32)], axis=0)
    xa_ref[:, 0:D_MODEL] = x1
    xa_ref[:, D_MODEL:AUG] = wrows.T


def _mix(alpha, nb, ns, cact, q, k, v, g, x, wco, bias, wao, wo, bo, l1g, l1b, wrt, br):
    ts = TOKEN_TILE
    nst = ns // ts
    t = nb * ns
    cur = lambda n: pl.BlockSpec((ts, n), lambda b, i: (b * nst + i, 0))
    prev = lambda n: pl.BlockSpec((ts, n), lambda b, i: (jnp.maximum(b * nst + i - 1, 0), 0))
    const = lambda shape: pl.BlockSpec(shape, lambda b, i: (0,) * len(shape))
    meta = pl.BlockSpec((1, 1, ts), lambda b, i: (b * nst + i, 0, 0))
    in_specs = [
        cur(D_CONV), cur(D_ATTN), cur(D_ATTN), prev(D_ATTN), cur(D_ATTN), prev(D_ATTN),
        cur(2 * D_MODEL), cur(D_MODEL),
        const((D_CONV, D_MODEL)), const((HEAD_PAIRS, 2 * CHUNK, BAND)), const((D_ATTN, D_MODEL)),
        const((D_MODEL, D_MODEL)), const((1, D_MODEL)), const((1, D_MODEL)), const((1, D_MODEL)),
        const((N_EXPERTS, D_MODEL)), const((N_EXPERTS, 1)),
    ]
    out_specs = [
        pl.BlockSpec((ts, AUG), lambda b, i: (b * nst + i, 0)),
        meta, meta,
        pl.BlockSpec((BUCKET_ROWS, LANES), lambda b, i: (0, 0)),
    ]
    out_shape = [
        jax.ShapeDtypeStruct((t, AUG), f32),
        jax.ShapeDtypeStruct((t // ts, 1, ts), i32),
        jax.ShapeDtypeStruct((t // ts, 1, ts), i32),
        jax.ShapeDtypeStruct((BUCKET_ROWS, LANES), f32),
    ]
    scratch = [
        pltpu.VMEM((2 * ts, D_ATTN), bf16),
        pltpu.VMEM((2 * ts, D_ATTN), bf16),
        pltpu.VMEM((ts, D_ATTN), bf16),
        pltpu.VMEM((2, HEAD_PAIRS, 2 * CHUNK, BAND), f32),
        pltpu.VMEM((2, HEAD_PAIRS, 2 * CHUNK, BAND), bf16),
        pltpu.VMEM((2, HEAD_PAIRS, 2 * CHUNK, 1), f32),
        pltpu.VMEM((ts, ts), bf16),
        pltpu.VMEM((BUCKET_ROWS, LANES), f32),
    ]
    return pl.pallas_call(
        functools.partial(_mix_kernel, alpha),
        grid=(nb, nst),
        in_specs=in_specs, out_specs=out_specs, out_shape=out_shape,
        scratch_shapes=scratch,
        compiler_params=pltpu.CompilerParams(
            dimension_semantics=("arbitrary", "arbitrary"), vmem_limit_bytes=VMEM_LIMIT),
        name="mix",
    )(cact, q, k, k, v, v, g, x, wco, bias, wao, wo, bo, l1g, l1b, wrt, br)


def _scatter_kernel(pad_start_ref, pad_len_ref, used_ref, pos_ref, xa_ref, xs_hbm, zbuf, sem):
    @pl.when(pl.program_id(0) == 0)
    def _():
        zbuf[...] = jnp.zeros_like(zbuf)
        tm = zbuf.shape[0]
        n_tiles = xs_hbm.shape[0] // tm
        for op in ("start", "wait"):
            def bucket_body(b, c, op=op):
                def row_body(r, c2):
                    getattr(_row_copy(zbuf, 0, xs_hbm, pad_start_ref[b] + r, sem), op)()
                    return c2
                lax.fori_loop(0, pad_len_ref[b], row_body, 0)
                return c
            lax.fori_loop(0, N_BUCKETS, bucket_body, 0)

            def tile_body(j, c, op=op):
                dst = xs_hbm.at[pl.ds(pl.multiple_of(j * tm, tm), tm)]
                getattr(pltpu.make_async_copy(zbuf, dst, sem), op)()
                return c
            lax.fori_loop(used_ref[0], n_tiles, tile_body, 0)

    _scatter_rows("start", pos_ref, xa_ref, xs_hbm, sem)
    _scatter_rows("wait", pos_ref, xa_ref, xs_hbm, sem)


def _scatter(pad_start, pad_len, used_tiles, pos, xa, n_rows):
    nt, _, ts = pos.shape
    grid_spec = pltpu.PrefetchScalarGridSpec(
        num_scalar_prefetch=3,
        grid=(nt,),
        in_specs=[pl.BlockSpec((1, 1, ts), lambda i, *_: (i, 0, 0), memory_space=pltpu.SMEM),
                  pl.BlockSpec((ts, AUG), lambda i, *_: (i, 0))],
        out_specs=pl.BlockSpec(memory_space=pl.ANY),
        scratch_shapes=[pltpu.VMEM((EXPERT_TILE, AUG), f32), pltpu.SemaphoreType.DMA(())],
    )
    return pl.pallas_call(
        _scatter_kernel,
        grid_spec=grid_spec,
        out_shape=jax.ShapeDtypeStruct((n_rows, AUG), f32),
        compiler_params=pltpu.CompilerParams(
            dimension_semantics=("arbitrary",), has_side_effects=True,
            vmem_limit_bytes=VMEM_LIMIT),
        name="scatter",
    )(pad_start, pad_len, used_tiles, pos, xa)


def _expert_kernel(alpha, elo_ref, ehi_ref, valid_ref, xs_ref, wgl_ref, wgh_ref, wdl_ref,
                   wdh_ref, g_ref, b_ref, ys_ref):
    j = pl.program_id(0)

    @pl.when(valid_ref[j] == 1)
    def _():
        x32 = xs_ref[:, 0:D_MODEL]
        x = x32.astype(bf16)
        y = alpha * x32
        for e, (wg_ref, wd_ref) in enumerate(((wgl_ref, wdl_ref), (wgh_ref, wdh_ref))):
            gu = jnp.dot(x, wg_ref[...], preferred_element_type=f32)
            gate = gu[:, 0:D_FF_EXPERT]
            up = gu[:, D_FF_EXPERT:2 * D_FF_EXPERT]
            h = (gate * _sigmoid(gate)) * up
            d = jnp.dot(h.astype(bf16), wd_ref[...], preferred_element_type=f32)
            y = y + xs_ref[:, D_MODEL + e:D_MODEL + e + 1] * d
        ys_ref[...] = _layer_norm(y, g_ref[...], b_ref[...])

    @pl.when(valid_ref[j] == 0)
    def _():
        ys_ref[...] = jnp.zeros_like(ys_ref)


def _experts(alpha, e_lo, e_hi, valid, xs, wgu, wd, g, b):
    n_rows = xs.shape[0]
    tm = EXPERT_TILE
    grid_spec = pltpu.PrefetchScalarGridSpec(
        num_scalar_prefetch=3,
        grid=(n_rows // tm,),
        in_specs=[
            pl.BlockSpec((tm, AUG), lambda j, lo, hi, ok: (j, 0)),
            pl.BlockSpec((None, D_MODEL, 2 * D_FF_EXPERT), lambda j, lo, hi, ok: (lo[j], 0, 0)),
            pl.BlockSpec((None, D_MODEL, 2 * D_FF_EXPERT), lambda j, lo, hi, ok: (hi[j], 0, 0)),
            pl.BlockSpec((None, D_FF_EXPERT, D_MODEL), lambda j, lo, hi, ok: (lo[j], 0, 0)),
            pl.BlockSpec((None, D_FF_EXPERT, D_MODEL), lambda j, lo, hi, ok: (hi[j], 0, 0)),
            pl.BlockSpec((1, D_MODEL), lambda j, lo, hi, ok: (0, 0)),
            pl.BlockSpec((1, D_MODEL), lambda j, lo, hi, ok: (0, 0)),
        ],
        out_specs=pl.BlockSpec((tm, D_MODEL), lambda j, lo, hi, ok: (j, 0)),
    )
    return pl.pallas_call(
        functools.partial(_expert_kernel, alpha),
        grid_spec=grid_spec,
        out_shape=jax.ShapeDtypeStruct((n_rows, D_MODEL), f32),
        compiler_params=pltpu.CompilerParams(
            dimension_semantics=("arbitrary",), vmem_limit_bytes=VMEM_LIMIT),
        name="experts",
    )(e_lo, e_hi, valid, xs, wgu, wgu, wd, wd, g, b)


def _gather_kernel(pos_ref, nxt_ref, ys_hbm, out_ref, buf, sem):
    slot = _gather_step(pos_ref, nxt_ref, ys_hbm, buf, sem)
    out_ref[...] = buf[slot]


def _gather(pos, ys):
    nt, _, ts = pos.shape
    cur, nxt = _pos_specs(nt, ts)
    return pl.pallas_call(
        _gather_kernel,
        grid=(nt,),
        in_specs=[cur, nxt, pl.BlockSpec(memory_space=pl.ANY)],
        out_specs=pl.BlockSpec((ts, D_MODEL), lambda i: (i, 0)),
        out_shape=jax.ShapeDtypeStruct((nt * ts, D_MODEL), f32),
        scratch_shapes=[pltpu.VMEM((2, ts, D_MODEL), f32), pltpu.SemaphoreType.DMA((2,))],
        compiler_params=pltpu.CompilerParams(
            dimension_semantics=("arbitrary",), vmem_limit_bytes=VMEM_LIMIT),
        name="gather",
    )(pos, pos, ys)


def _bucket_tables():
    lo = np.array([g * EXPERTS_PER_GROUP + a for g in range(N_GROUPS) for a, _ in PAIRS], np.int32)
    hi = np.array([g * EXPERTS_PER_GROUP + b for g in range(N_GROUPS) for _, b in PAIRS], np.int32)
    return lo, hi


def _score_bias(rel_bias):
    diag = np.arange(CHUNK - 1 + BAND)
    idx = np.clip(LEFT + CHUNK - 1 - diag, -MAX_REL, MAX_REL) + MAX_REL
    ext = (rel_bias * LOG2E)[:, idx]
    rows = [ext[:, CHUNK - 1 - i:CHUNK - 1 - i + BAND] for i in range(CHUNK)]
    return jnp.stack(rows, axis=1).reshape(HEAD_PAIRS, 2 * CHUNK, BAND)


def _table_lookup(table, idx):
    out = jnp.zeros(idx.shape, table.dtype)
    for n in range(table.shape[0]):
        out = out + jnp.where(idx == n, table[n], 0)
    return out


def kernel(x, w_in, b_in, w_dw, b_dw, ln_conv_g, ln_conv_b, w_conv_out, rel_bias, w_attn_out,
           w_o, b_o, ln1_g, ln1_b, w_router, b_router, w_gate_up, w_down, ln2_g, ln2_b):
    nb, ns, d = x.shape
    depth = w_in.shape[0]
    assert d == D_MODEL and ns % TOKEN_TILE == 0 and TOKEN_TILE >= LEFT
    alpha = (2.0 * depth) ** 0.25
    t = nb * ns
    tiles_per_seq = ns // TOKEN_TILE
    n_tiles = t // EXPERT_TILE + N_BUCKETS
    n_rows = n_tiles * EXPERT_TILE
    lo_np, hi_np = _bucket_tables()
    lo_tab, hi_tab = jnp.asarray(lo_np), jnp.asarray(hi_np)
    row = lambda a: a.reshape(1, -1)

    wrt = w_router.T
    br = b_router.reshape(N_EXPERTS, 1)
    h = x.reshape(t, d)
    pos = ys = None
    for l in range(depth):
        wdw = jnp.zeros((HALO_ROWS, D_CONV), f32).at[:CONV_WIDTH].set(w_dw[l])
        proj_weights = (w_in[l].astype(bf16), row(b_in[l]), wdw, row(b_dw[l]),
                        row(ln_conv_g[l]), row(ln_conv_b[l]))
        if l == 0:
            cact, q, k, v, g = _proj(tiles_per_seq, h, *proj_weights)
        else:
            h, cact, q, k, v, g = _proj_gather(tiles_per_seq, pos, ys, *proj_weights)
        xa, bucket, rank, counts = _mix(
            alpha, nb, ns, cact, q, k, v, g, h, w_conv_out[l].astype(bf16),
            _score_bias(rel_bias[l]), w_attn_out[l].astype(bf16), w_o[l].astype(bf16),
            row(b_o[l]), row(ln1_g[l]), row(ln1_b[l]), wrt, br)

        cnt = counts[:N_BUCKETS, 0].astype(i32)
        tiles = (cnt + EXPERT_TILE - 1) // EXPERT_TILE
        tile_end = jnp.cumsum(tiles)
        row_start = (tile_end - tiles) * EXPERT_TILE
        pos = _table_lookup(row_start, bucket) + rank
        tile_ids = jnp.arange(n_tiles, dtype=i32)
        tile_bucket = jnp.minimum(
            jnp.sum((tile_ids[:, None] >= tile_end[None, :]).astype(i32), axis=1), N_BUCKETS - 1)
        valid = (tile_ids < tile_end[-1]).astype(i32)

        xs = _scatter(row_start + cnt, tiles * EXPERT_TILE - cnt, tile_end[-1:], pos, xa, n_rows)
        ys = _experts(alpha, _table_lookup(lo_tab, tile_bucket), _table_lookup(hi_tab, tile_bucket),
                      valid, xs, w_gate_up[l].astype(bf16), w_down[l].astype(bf16),
                      row(ln2_g[l]), row(ln2_b[l]))
    return _gather(pos, ys).reshape(nb, ns, d)
```

```python
import functools
import math

import jax
import jax.numpy as jnp
import numpy as np
from jax import lax
from jax.experimental import pallas as pl
from jax.experimental.pallas import tpu as pltpu

f32 = jnp.float32
bf16 = jnp.bfloat16
i32 = jnp.int32

D_MODEL = 1024
CHUNK = 64
N_LEFT_CHUNKS = 8
LEFT = N_LEFT_CHUNKS * CHUNK
BAND = LEFT + CHUNK
ATTN_HEADS = 8
HEAD_DIM = 64
D_ATTN = ATTN_HEADS * HEAD_DIM
MAX_REL = 256
D_CONV = D_MODEL - D_ATTN
CONV_WIDTH = 31
D_IN = 2 * D_CONV + 3 * D_ATTN + 2 * D_MODEL
N_EXPERTS = 16
N_GROUPS = 4
EXPERTS_PER_GROUP = N_EXPERTS // N_GROUPS
D_FF_EXPERT = 512
LN_EPS = 1e-5
LOG2E = math.log2(math.e)

PAIRS = [(a, b) for a in range(EXPERTS_PER_GROUP) for b in range(a + 1, EXPERTS_PER_GROUP)]
N_PAIRS = len(PAIRS)
N_BUCKETS = N_GROUPS * N_PAIRS
BUCKET_ROWS = 32

LANES = 128
SUBLANES = 8
TOKEN_TILE = 512
EXPERT_TILE = 256
HALO_ROWS = 32
CONV_ROWS = 128
HEAD_PAIRS = ATTN_HEADS // 2
AUG = D_MODEL + LANES
DMA_UNROLL = 8
GATHER_GROUPS = 8
VMEM_LIMIT = 56 * 1024 * 1024
NEG_BIG = -1e30


def _layer_norm(x, g, b):
    mu = jnp.mean(x, axis=-1, keepdims=True)
    xc = x - mu
    var = jnp.mean(xc * xc, axis=-1, keepdims=True)
    return xc * lax.rsqrt(var + LN_EPS) * g + b


def _sigmoid(x):
    return 1.0 / (1.0 + jnp.exp(-x))


def _row_copy(src, src_row, dst, dst_row, sem):
    return pltpu.make_async_copy(src.at[pl.ds(src_row, 1)], dst.at[pl.ds(dst_row, 1)], sem)


def _for_rows(n, body, op):
    if op == "start":
        for r in range(n):
            body(r)
    else:
        def step(r, c):
            body(r)
            return c
        lax.fori_loop(0, n, step, 0, unroll=DMA_UNROLL)


def _gather_rows(op, idx_ref, src_hbm, dst, sem):
    def body(r):
        getattr(_row_copy(src_hbm, idx_ref[0, 0, r], dst, r, sem), op)()
    _for_rows(idx_ref.shape[-1], body, op)


def _scatter_rows(op, idx_ref, src, dst_hbm, sem):
    def body(r):
        getattr(_row_copy(src, r, dst_hbm, idx_ref[0, 0, r], sem), op)()
    _for_rows(idx_ref.shape[-1], body, op)


def _gather_step(pos_ref, nxt_ref, src_hbm, buf, sem):
    i = pl.program_id(0)
    slot = lax.rem(i, 2)

    @pl.when(i == 0)
    def _():
        _gather_rows("start", pos_ref, src_hbm, buf.at[0], sem.at[0])

    @pl.when(i + 1 < pl.num_programs(0))
    def _():
        _gather_rows("start", nxt_ref, src_hbm, buf.at[1 - slot], sem.at[1 - slot])

    _gather_rows("wait", pos_ref, src_hbm, buf.at[slot], sem.at[slot])
    return slot


def _pos_specs(nt, ts):
    cur = pl.BlockSpec((1, 1, ts), lambda i: (i, 0, 0), memory_space=pltpu.SMEM)
    nxt = pl.BlockSpec((1, 1, ts), lambda i: (jnp.minimum(i + 1, nt - 1), 0, 0),
                       memory_space=pltpu.SMEM)
    return cur, nxt


def _conv_branch(u, first_of_seq, wdw_ref, bdw_ref, lcg_ref, lcb_ref, c_ref, uext, ushift):
    tm = u.shape[0]

    @pl.when(first_of_seq)
    def _():
        uext[0:HALO_ROWS, :] = jnp.zeros((HALO_ROWS, D_CONV), f32)

    @pl.when(jnp.logical_not(first_of_seq))
    def _():
        uext[0:HALO_ROWS, :] = uext[tm:tm + HALO_ROWS, :]

    uext[HALO_ROWS:HALO_ROWS + tm, :] = u
    span = tm + HALO_ROWS - SUBLANES
    for p in range(1, SUBLANES):
        ushift[p - 1, 0:span, :] = uext[p:p + span, :]
    bdw = bdw_ref[...]
    lcg = lcg_ref[...]
    lcb = lcb_ref[...]
    first_tap = HALO_ROWS - (CONV_WIDTH - 1)

    def conv_rows(rc):
        r0 = rc * CONV_ROWS
        blocks = []
        for cb in range(D_CONV // LANES):
            cl = slice(cb * LANES, (cb + 1) * LANES)
            acc = jnp.zeros((CONV_ROWS, LANES), f32)
            for w in range(CONV_WIDTH):
                off = first_tap + w
                p = off % SUBLANES
                base = r0 + off - p
                if p == 0:
                    rows = uext[base:base + CONV_ROWS, cl]
                else:
                    rows = ushift[p - 1, base:base + CONV_ROWS, cl]
                acc = acc + rows * wdw_ref[w:w + 1, cl]
            blocks.append(acc)
        y = _layer_norm(jnp.concatenate(blocks, axis=1) + bdw, lcg, lcb)
        y = y * _sigmoid(y)
        c_ref[r0:r0 + CONV_ROWS, :] = y.astype(bf16)

    return [functools.partial(conv_rows, rc) for rc in range(tm // CONV_ROWS)]


def _proj_body(tiles_per_seq, x, extra_steps, w_ref, b_ref, conv_refs, out_refs, conv_scratch):
    c_ref, q_ref, k_ref, v_ref, g_ref = out_refs
    x = x.astype(bf16)

    def seg(lo, hi):
        return jnp.dot(x, w_ref[:, lo:hi], preferred_element_type=f32) + b_ref[:, lo:hi]

    o = 0
    a = seg(o, o + D_CONV)
    gate = seg(o + D_CONV, o + 2 * D_CONV)
    first_of_seq = lax.rem(pl.program_id(0), tiles_per_seq) == 0
    conv_steps = _conv_branch(a * _sigmoid(gate), first_of_seq, *conv_refs, c_ref, *conv_scratch)
    o += 2 * D_CONV

    def q_step(o=o):
        q_ref[...] = (seg(o, o + D_ATTN) * (HEAD_DIM ** -0.5 * LOG2E)).astype(bf16)

    def copy_step(ref, o):
        ref[...] = seg(o, o + D_ATTN).astype(bf16)

    def gate_step(n, o):
        g_ref[:, n * D_ATTN:(n + 1) * D_ATTN] = _sigmoid(seg(o, o + D_ATTN)).astype(bf16)

    matmul_steps = [q_step,
                    functools.partial(copy_step, k_ref, o + D_ATTN),
                    functools.partial(copy_step, v_ref, o + 2 * D_ATTN)]
    o += 3 * D_ATTN
    matmul_steps += [functools.partial(gate_step, n, o + n * D_ATTN)
                     for n in range(2 * D_MODEL // D_ATTN)]
    for n in range(max(len(conv_steps), len(matmul_steps), len(extra_steps))):
        for steps in (conv_steps, extra_steps, matmul_steps):
            if n < len(steps):
                steps[n]()


def _proj_kernel(tiles_per_seq, x_ref, w_ref, b_ref, *rest):
    conv_refs, out_refs, conv_scratch = rest[:4], rest[4:9], rest[9:]
    _proj_body(tiles_per_seq, x_ref[...], (), w_ref, b_ref, conv_refs, out_refs, conv_scratch)


def _proj_gather_kernel(tiles_per_seq, pos_ref, nxt_ref, ys_hbm, w_ref, b_ref, *rest):
    conv_refs, x_ref, out_refs = rest[:4], rest[4], rest[5:10]
    buf, sem, conv_scratch = rest[10], rest[11], rest[12:]
    i = pl.program_id(0)
    tm = x_ref.shape[0]
    slot = lax.rem(i, 2)

    @pl.when(i == 0)
    def _():
        _gather_rows("start", pos_ref, ys_hbm, buf.at[0], sem.at[0])

    _gather_rows("wait", pos_ref, ys_hbm, buf.at[slot], sem.at[slot])

    def start_rows(lo):
        for r in range(lo, lo + tm // GATHER_GROUPS):
            _row_copy(ys_hbm, nxt_ref[0, 0, r], buf.at[1 - slot], r, sem.at[1 - slot]).start()

    start_steps = [functools.partial(start_rows, n * (tm // GATHER_GROUPS))
                   for n in range(GATHER_GROUPS)]
    x = buf[slot]
    x_ref[...] = x
    _proj_body(tiles_per_seq, x, start_steps, w_ref, b_ref, conv_refs, out_refs, conv_scratch)

    @pl.when(i == pl.num_programs(0) - 1)
    def _():
        _gather_rows("wait", nxt_ref, ys_hbm, buf.at[1 - slot], sem.at[1 - slot])


_PROJ_WIDTHS = (D_CONV, D_ATTN, D_ATTN, D_ATTN, 2 * D_MODEL)


def _proj_common(tm):
    row = lambda n: pl.BlockSpec((tm, n), lambda i: (i, 0))
    const = lambda shape: pl.BlockSpec(shape, lambda i: (0,) * len(shape))
    weight_specs = [const((D_MODEL, D_IN)), const((1, D_IN)), const((HALO_ROWS, D_CONV)),
                    const((1, D_CONV)), const((1, D_CONV)), const((1, D_CONV))]
    conv_scratch = [pltpu.VMEM((HALO_ROWS + tm, D_CONV), f32),
                    pltpu.VMEM((SUBLANES - 1, HALO_ROWS + tm, D_CONV), f32)]
    params = pltpu.CompilerParams(dimension_semantics=("arbitrary",), vmem_limit_bytes=VMEM_LIMIT)
    return row, weight_specs, conv_scratch, params


def _proj(tiles_per_seq, x, w_in, b_in, wdw, bdw, lcg, lcb):
    t = x.shape[0]
    tm = TOKEN_TILE
    row, weight_specs, conv_scratch, params = _proj_common(tm)
    return pl.pallas_call(
        functools.partial(_proj_kernel, tiles_per_seq),
        grid=(t // tm,),
        in_specs=[row(D_MODEL)] + weight_specs,
        out_specs=[row(n) for n in _PROJ_WIDTHS],
        out_shape=[jax.ShapeDtypeStruct((t, n), bf16) for n in _PROJ_WIDTHS],
        scratch_shapes=conv_scratch,
        compiler_params=params,
        name="proj",
    )(x, w_in, b_in, wdw, bdw, lcg, lcb)


def _proj_gather(tiles_per_seq, pos, ys, w_in, b_in, wdw, bdw, lcg, lcb):
    nt, _, tm = pos.shape
    t = nt * tm
    row, weight_specs, conv_scratch, params = _proj_common(tm)
    cur, nxt = _pos_specs(nt, tm)
    return pl.pallas_call(
        functools.partial(_proj_gather_kernel, tiles_per_seq),
        grid=(nt,),
        in_specs=[cur, nxt, pl.BlockSpec(memory_space=pl.ANY)] + weight_specs,
        out_specs=[row(D_MODEL)] + [row(n) for n in _PROJ_WIDTHS],
        out_shape=[jax.ShapeDtypeStruct((t, D_MODEL), f32)]
                  + [jax.ShapeDtypeStruct((t, n), bf16) for n in _PROJ_WIDTHS],
        scratch_shapes=[pltpu.VMEM((2, tm, D_MODEL), f32), pltpu.SemaphoreType.DMA((2,))]
                       + conv_scratch,
        compiler_params=params,
        name="proj_gather",
    )(pos, pos, ys, w_in, b_in, wdw, bdw, lcg, lcb)


def _route(x1, wrt_ref, br_ref, tri_ref, run_ref):
    ts = x1.shape[0]
    logits = lax.dot_general(wrt_ref[...], x1.astype(bf16), (((1,), (1,)), ((), ())),
                             preferred_element_type=f32) + br_ref[...]
    mx = jnp.max(logits, axis=0, keepdims=True)
    ex = jnp.exp(logits - mx)
    probs = ex / jnp.sum(ex, axis=0, keepdims=True)

    best = None
    for g in range(N_GROUPS):
        r = [probs[g * EXPERTS_PER_GROUP + j:g * EXPERTS_PER_GROUP + j + 1, :]
             for j in range(EXPERTS_PER_GROUP)]
        v1 = jnp.maximum(jnp.maximum(r[0], r[1]), jnp.maximum(r[2], r[3]))
        i1 = jnp.where(r[0] == v1, 0, jnp.where(r[1] == v1, 1, jnp.where(r[2] == v1, 2, 3)))
        rm = [jnp.where(i1 == j, -1.0, r[j]) for j in range(EXPERTS_PER_GROUP)]
        v2 = jnp.maximum(jnp.maximum(rm[0], rm[1]), jnp.maximum(rm[2], rm[3]))
        i2 = jnp.where(rm[0] == v2, 0, jnp.where(rm[1] == v2, 1, jnp.where(rm[2] == v2, 2, 3)))
        cand = (v1 + v2, v1, v2, i1, i2, jnp.full_like(i1, g))
        if best is None:
            best = cand
        else:
            take = cand[0] > best[0]
            best = tuple(jnp.where(take, c, b) for c, b in zip(cand, best))
    _, v1, v2, i1, i2, grp = best
    den = v1 + v2
    w1 = v1 / den
    w2 = v2 / den
    first_is_lo = i1 < i2
    lo = jnp.where(first_is_lo, i1, i2)
    hi = jnp.where(first_is_lo, i2, i1)
    w_lo = jnp.where(first_is_lo, w1, w2)
    w_hi = jnp.where(first_is_lo, w2, w1)
    pair_base = jnp.where(lo == 0, 0, jnp.where(lo == 1, 3, 5))
    bucket = grp * N_PAIRS + pair_base + (hi - lo - 1)

    rows = lax.broadcasted_iota(i32, (BUCKET_ROWS, ts), 0)
    onehot = (rows == bucket).astype(f32)
    before = jnp.dot(onehot.astype(bf16), tri_ref[...], preferred_element_type=f32)
    rank = jnp.sum(onehot * (before + run_ref[:, 0:1]), axis=0, keepdims=True)
    run_ref[...] = run_ref[...] + jnp.sum(onehot, axis=1, keepdims=True)
    return bucket, rank.astype(i32), w_lo, w_hi


def _attention(first_tile, q_ref, kbuf, vbuf, bias_ref, abuf, sbuf, pbuf, dbuf):
    ts = q_ref.shape[0]
    n_chunks = ts // CHUNK
    col = lax.broadcasted_iota(i32, (2 * CHUNK, BAND), 1)
    first_head = lax.broadcasted_iota(i32, (CHUNK, 2 * HEAD_DIM), 1) < HEAD_DIM
    lanes = [slice(hp * 2 * HEAD_DIM, (hp + 1) * 2 * HEAD_DIM) for hp in range(HEAD_PAIRS)]

    def scores(c):
        for hp in range(HEAD_PAIRS):
            q2 = q_ref[c * CHUNK:(c + 1) * CHUNK, lanes[hp]]
            k2 = kbuf[c * CHUNK:c * CHUNK + BAND, lanes[hp]]
            zero = jnp.zeros_like(q2)
            qq = jnp.concatenate([jnp.where(first_head, q2, zero),
                                  jnp.where(first_head, zero, q2)], axis=0)
            sbuf[c % 2, hp] = lax.dot_general(qq, k2, (((1,), (1,)), ((), ())),
                                              preferred_element_type=f32)

    def softmax(c):
        for hp in range(HEAD_PAIRS):
            s = sbuf[c % 2, hp] + bias_ref[hp]
            if first_tile:
                s = jnp.where(col >= LEFT - c * CHUNK, s, NEG_BIG)
            p = jnp.exp2(s - jnp.max(s, axis=-1, keepdims=True))
            dbuf[c % 2, hp] = jnp.sum(p, axis=-1, keepdims=True)
            pbuf[c % 2, hp] = p.astype(bf16)

    def values(c):
        for hp in range(HEAD_PAIRS):
            v2 = vbuf[c * CHUNK:c * CHUNK + BAND, lanes[hp]]
            o = jnp.dot(pbuf[c % 2, hp], v2, preferred_element_type=f32) / dbuf[c % 2, hp]
            o = jnp.where(first_head, o[0:CHUNK], o[CHUNK:2 * CHUNK])
            abuf[c * CHUNK:(c + 1) * CHUNK, lanes[hp]] = o.astype(bf16)

    scores(0)
    for c in range(n_chunks):
        if c + 1 < n_chunks:
            scores(c + 1)
        softmax(c)
        if c >= 1:
            values(c - 1)
    values(n_chunks - 1)


def _mix_kernel(alpha,
                c_ref, q_ref, kc_ref, kp_ref, vc_ref, vp_ref, g_ref, x_ref,
                wco_ref, bias_ref, wao_ref, wo_ref, bo_ref, l1g_ref, l1b_ref, wrt_ref, br_ref,
                xa_ref, bucket_ref, rank_ref, counts_ref,
                kbuf, vbuf, abuf, sbuf, pbuf, dbuf, tri_ref, run_ref):
    b = pl.program_id(0)
    i = pl.program_id(1)
    ts = q_ref.shape[0]

    @pl.when(jnp.logical_and(b == 0, i == 0))
    def _():
        run_ref[...] = jnp.zeros_like(run_ref)
        s = lax.broadcasted_iota(i32, (ts, ts), 0)
        t = lax.broadcasted_iota(i32, (ts, ts), 1)
        tri_ref[...] = jnp.where(s < t, 1.0, 0.0).astype(bf16)

    kbuf[0:ts, :] = kp_ref[...]
    kbuf[ts:2 * ts, :] = kc_ref[...]
    vbuf[0:ts, :] = vp_ref[...]
    vbuf[ts:2 * ts, :] = vc_ref[...]
    for first_tile in (True, False):
        @pl.when((i == 0) == first_tile)
        def _(first_tile=first_tile):
            _attention(first_tile, q_ref, kbuf, vbuf, bias_ref, abuf, sbuf, pbuf, dbuf)

    y_conv = jnp.dot(c_ref[...], wco_ref[...], preferred_element_type=f32)
    y_attn = jnp.dot(abuf[...], wao_ref[...], preferred_element_type=f32)
    merged = (g_ref[:, 0:D_MODEL] * y_conv.astype(bf16)
              + g_ref[:, D_MODEL:2 * D_MODEL] * y_attn.astype(bf16))
    mixed = jnp.dot(merged, wo_ref[...], preferred_element_type=f32) + bo_ref[...]
    x1 = _layer_norm(alpha * x_ref[...] + mixed, l1g_ref[...], l1b_ref[...])

    bucket, rank, w_lo, w_hi = _route(x1, wrt_ref, br_ref, tri_ref, run_ref)
    bucket_ref[0] = bucket
    rank_ref[0] = rank
    counts_ref[...] = run_ref[...]
    wrows = jnp.concatenate([w_lo, w_hi, jnp.zeros((LANES - 2, ts), f32)], axis=0)
    xa_ref[:, 0:D_MODEL] = x1
    xa_ref[:, D_MODEL:AUG] = wrows.T


def _mix(alpha, nb, ns, cact, q, k, v, g, x, wco, bias, wao, wo, bo, l1g, l1b, wrt, br):
    ts = TOKEN_TILE
    nst = ns // ts
    t = nb * ns
    cur = lambda n: pl.BlockSpec((ts, n), lambda b, i: (b * nst + i, 0))
    prev = lambda n: pl.BlockSpec((ts, n), lambda b, i: (jnp.maximum(b * nst + i - 1, 0), 0))
    const = lambda shape: pl.BlockSpec(shape, lambda b, i: (0,) * len(shape))
    meta = pl.BlockSpec((1, 1, ts), lambda b, i: (b * nst + i, 0, 0))
    in_specs = [
        cur(D_CONV), cur(D_ATTN), cur(D_ATTN), prev(D_ATTN), cur(D_ATTN), prev(D_ATTN),
        cur(2 * D_MODEL), cur(D_MODEL),
        const((D_CONV, D_MODEL)), const((HEAD_PAIRS, 2 * CHUNK, BAND)), const((D_ATTN, D_MODEL)),
        const((D_MODEL, D_MODEL)), const((1, D_MODEL)), const((1, D_MODEL)), const((1, D_MODEL)),
        const((N_EXPERTS, D_MODEL)), const((N_EXPERTS, 1)),
    ]
    out_specs = [
        pl.BlockSpec((ts, AUG), lambda b, i: (b * nst + i, 0)),
        meta, meta,
        pl.BlockSpec((BUCKET_ROWS, LANES), lambda b, i: (0, 0)),
    ]
    out_shape = [
        jax.ShapeDtypeStruct((t, AUG), f32),
        jax.ShapeDtypeStruct((t // ts, 1, ts), i32),
        jax.ShapeDtypeStruct((t // ts, 1, ts), i32),
        jax.ShapeDtypeStruct((BUCKET_ROWS, LANES), f32),
    ]
    scratch = [
        pltpu.VMEM((2 * ts, D_ATTN), bf16),
        pltpu.VMEM((2 * ts, D_ATTN), bf16),
        pltpu.VMEM((ts, D_ATTN), bf16),
        pltpu.VMEM((2, HEAD_PAIRS, 2 * CHUNK, BAND), f32),
        pltpu.VMEM((2, HEAD_PAIRS, 2 * CHUNK, BAND), bf16),
        pltpu.VMEM((2, HEAD_PAIRS, 2 * CHUNK, 1), f32),
        pltpu.VMEM((ts, ts), bf16),
        pltpu.VMEM((BUCKET_ROWS, LANES), f32),
    ]
    return pl.pallas_call(
        functools.partial(_mix_kernel, alpha),
        grid=(nb, nst),
        in_specs=in_specs, out_specs=out_specs, out_shape=out_shape,
        scratch_shapes=scratch,
        compiler_params=pltpu.CompilerParams(
            dimension_semantics=("arbitrary", "arbitrary"), vmem_limit_bytes=VMEM_LIMIT),
        name="mix",
    )(cact, q, k, k, v, v, g, x, wco, bias, wao, wo, bo, l1g, l1b, wrt, br)


def _scatter_kernel(pad_start_ref, pad_len_ref, used_ref, pos_ref, xa_ref, xs_hbm, zbuf, sem):
    @pl.when(pl.program_id(0) == 0)
    def _():
        zbuf[...] = jnp.zeros_like(zbuf)
        tm = zbuf.shape[0]
        n_tiles = xs_hbm.shape[0] // tm
        for op in ("start", "wait"):
            def bucket_body(b, c, op=op):
                def row_body(r, c2):
                    getattr(_row_copy(zbuf, 0, xs_hbm, pad_start_ref[b] + r, sem), op)()
                    return c2
                lax.fori_loop(0, pad_len_ref[b], row_body, 0)
                return c
            lax.fori_loop(0, N_BUCKETS, bucket_body, 0)

            def tile_body(j, c, op=op):
                dst = xs_hbm.at[pl.ds(pl.multiple_of(j * tm, tm), tm)]
                getattr(pltpu.make_async_copy(zbuf, dst, sem), op)()
                return c
            lax.fori_loop(used_ref[0], n_tiles, tile_body, 0)

    _scatter_rows("start", pos_ref, xa_ref, xs_hbm, sem)
    _scatter_rows("wait", pos_ref, xa_ref, xs_hbm, sem)


def _scatter(pad_start, pad_len, used_tiles, pos, xa, n_rows):
    nt, _, ts = pos.shape
    grid_spec = pltpu.PrefetchScalarGridSpec(
        num_scalar_prefetch=3,
        grid=(nt,),
        in_specs=[pl.BlockSpec((1, 1, ts), lambda i, *_: (i, 0, 0), memory_space=pltpu.SMEM),
                  pl.BlockSpec((ts, AUG), lambda i, *_: (i, 0))],
        out_specs=pl.BlockSpec(memory_space=pl.ANY),
        scratch_shapes=[pltpu.VMEM((EXPERT_TILE, AUG), f32), pltpu.SemaphoreType.DMA(())],
    )
    return pl.pallas_call(
        _scatter_kernel,
        grid_spec=grid_spec,
        out_shape=jax.ShapeDtypeStruct((n_rows, AUG), f32),
        compiler_params=pltpu.CompilerParams(
            dimension_semantics=("arbitrary",), has_side_effects=True,
            vmem_limit_bytes=VMEM_LIMIT),
        name="scatter",
    )(pad_start, pad_len, used_tiles, pos, xa)


def _expert_kernel(alpha, elo_ref, ehi_ref, valid_ref, xs_ref, wgl_ref, wgh_ref, wdl_ref,
                   wdh_ref, g_ref, b_ref, ys_ref):
    j = pl.program_id(0)

    @pl.when(valid_ref[j] == 1)
    def _():
        x32 = xs_ref[:, 0:D_MODEL]
        x = x32.astype(bf16)
        y = alpha * x32
        for e, (wg_ref, wd_ref) in enumerate(((wgl_ref, wdl_ref), (wgh_ref, wdh_ref))):
            gu = jnp.dot(x, wg_ref[...], preferred_element_type=f32)
            gate = gu[:, 0:D_FF_EXPERT]
            up = gu[:, D_FF_EXPERT:2 * D_FF_EXPERT]
            h = (gate * _sigmoid(gate)) * up
            d = jnp.dot(h.astype(bf16), wd_ref[...], preferred_element_type=f32)
            y = y + xs_ref[:, D_MODEL + e:D_MODEL + e + 1] * d
        ys_ref[...] = _layer_norm(y, g_ref[...], b_ref[...])

    @pl.when(valid_ref[j] == 0)
    def _():
        ys_ref[...] = jnp.zeros_like(ys_ref)


def _experts(alpha, e_lo, e_hi, valid, xs, wgu, wd, g, b):
    n_rows = xs.shape[0]
    tm = EXPERT_TILE
    grid_spec = pltpu.PrefetchScalarGridSpec(
        num_scalar_prefetch=3,
        grid=(n_rows // tm,),
        in_specs=[
            pl.BlockSpec((tm, AUG), lambda j, lo, hi, ok: (j, 0)),
            pl.BlockSpec((None, D_MODEL, 2 * D_FF_EXPERT), lambda j, lo, hi, ok: (lo[j], 0, 0)),
            pl.BlockSpec((None, D_MODEL, 2 * D_FF_EXPERT), lambda j, lo, hi, ok: (hi[j], 0, 0)),
            pl.BlockSpec((None, D_FF_EXPERT, D_MODEL), lambda j, lo, hi, ok: (lo[j], 0, 0)),
            pl.BlockSpec((None, D_FF_EXPERT, D_MODEL), lambda j, lo, hi, ok: (hi[j], 0, 0)),
            pl.BlockSpec((1, D_MODEL), lambda j, lo, hi, ok: (0, 0)),
            pl.BlockSpec((1, D_MODEL), lambda j, lo, hi, ok: (0, 0)),
        ],
        out_specs=pl.BlockSpec((tm, D_MODEL), lambda j, lo, hi, ok: (j, 0)),
    )
    return pl.pallas_call(
        functools.partial(_expert_kernel, alpha),
        grid_spec=grid_spec,
        out_shape=jax.ShapeDtypeStruct((n_rows, D_MODEL), f32),
        compiler_params=pltpu.CompilerParams(
            dimension_semantics=("arbitrary",), vmem_limit_bytes=VMEM_LIMIT),
        name="experts",
    )(e_lo, e_hi, valid, xs, wgu, wgu, wd, wd, g, b)


def _gather_kernel(pos_ref, nxt_ref, ys_hbm, out_ref, buf, sem):
    slot = _gather_step(pos_ref, nxt_ref, ys_hbm, buf, sem)
    out_ref[...] = buf[slot]


def _gather(pos, ys):
    nt, _, ts = pos.shape
    cur, nxt = _pos_specs(nt, ts)
    return pl.pallas_call(
        _gather_kernel,
        grid=(nt,),
        in_specs=[cur, nxt, pl.BlockSpec(memory_space=pl.ANY)],
        out_specs=pl.BlockSpec((ts, D_MODEL), lambda i: (i, 0)),
        out_shape=jax.ShapeDtypeStruct((nt * ts, D_MODEL), f32),
        scratch_shapes=[pltpu.VMEM((2, ts, D_MODEL), f32), pltpu.SemaphoreType.DMA((2,))],
        compiler_params=pltpu.CompilerParams(
            dimension_semantics=("arbitrary",), vmem_limit_bytes=VMEM_LIMIT),
        name="gather",
    )(pos, pos, ys)


def _bucket_tables():
    lo = np.array([g * EXPERTS_PER_GROUP + a for g in range(N_GROUPS) for a, _ in PAIRS], np.int32)
    hi = np.array([g * EXPERTS_PER_GROUP + b for g in range(N_GROUPS) for _, b in PAIRS], np.int32)
    return lo, hi


def _score_bias(rel_bias):
    diag = np.arange(CHUNK - 1 + BAND)
    idx = np.clip(LEFT + CHUNK - 1 - diag, -MAX_REL, MAX_REL) + MAX_REL
    ext = (rel_bias * LOG2E)[:, idx]
    rows = [ext[:, CHUNK - 1 - i:CHUNK - 1 - i + BAND] for i in range(CHUNK)]
    return jnp.stack(rows, axis=1).reshape(HEAD_PAIRS, 2 * CHUNK, BAND)


def _table_lookup(table, idx):
    out = jnp.zeros(idx.shape, table.dtype)
    for n in range(table.shape[0]):
        out = out + jnp.where(idx == n, table[n], 0)
    return out


def kernel(x, w_in, b_in, w_dw, b_dw, ln_conv_g, ln_conv_b, w_conv_out, rel_bias, w_attn_out,
           w_o, b_o, ln1_g, ln1_b, w_router, b_router, w_gate_up, w_down, ln2_g, ln2_b):
    nb, ns, d = x.shape
    depth = w_in.shape[0]
    assert d == D_MODEL and ns % TOKEN_TILE == 0 and TOKEN_TILE >= LEFT
    alpha = (2.0 * depth) ** 0.25
    t = nb * ns
    tiles_per_seq = ns // TOKEN_TILE
    n_tiles = t // EXPERT_TILE + N_BUCKETS
    n_rows = n_tiles * EXPERT_TILE
    lo_np, hi_np = _bucket_tables()
    lo_tab, hi_tab = jnp.asarray(lo_np), jnp.asarray(hi_np)
    row = lambda a: a.reshape(1, -1)

    wrt = w_router.T.astype(bf16)
    br = b_router.reshape(N_EXPERTS, 1)
    h = x.reshape(t, d)
    pos = ys = None
    for l in range(depth):
        wdw = jnp.zeros((HALO_ROWS, D_CONV), f32).at[:CONV_WIDTH].set(w_dw[l])
        proj_weights = (w_in[l].astype(bf16), row(b_in[l]), wdw, row(b_dw[l]),
                        row(ln_conv_g[l]), row(ln_conv_b[l]))
        if l == 0:
            cact, q, k, v, g = _proj(tiles_per_seq, h, *proj_weights)
        else:
            h, cact, q, k, v, g = _proj_gather(tiles_per_seq, pos, ys, *proj_weights)
        xa, bucket, rank, counts = _mix(
            alpha, nb, ns, cact, q, k, v, g, h, w_conv_out[l].astype(bf16),
            _score_bias(rel_bias[l]), w_attn_out[l].astype(bf16), w_o[l].astype(bf16),
            row(b_o[l]), row(ln1_g[l]), row(ln1_b[l]), wrt, br)

        cnt = counts[:N_BUCKETS, 0].astype(i32)
        tiles = (cnt + EXPERT_TILE - 1) // EXPERT_TILE
        tile_end = jnp.cumsum(tiles)
        row_start = (tile_end - tiles) * EXPERT_TILE
        pos = _table_lookup(row_start, bucket) + rank
        tile_ids = jnp.arange(n_tiles, dtype=i32)
        tile_bucket = jnp.minimum(
            jnp.sum((tile_ids[:, None] >= tile_end[None, :]).astype(i32), axis=1), N_BUCKETS - 1)
        valid = (tile_ids < tile_end[-1]).astype(i32)

        xs = _scatter(row_start + cnt, tiles * EXPERT_TILE - cnt, tile_end[-1:], pos, xa, n_rows)
        ys = _experts(alpha, _table_lookup(lo_tab, tile_bucket), _table_lookup(hi_tab, tile_bucket),
                      valid, xs, w_gate_up[l].astype(bf16), w_down[l].astype(bf16),
                      row(ln2_g[l]), row(ln2_b[l]))
    return _gather(pos, ys).reshape(nb, ns, d)
```

```python
import functools
import math

import jax
import jax.numpy as jnp
import numpy as np
from jax import lax
from jax.experimental import pallas as pl
from jax.experimental.pallas import tpu as pltpu

f32 = jnp.float32
bf16 = jnp.bfloat16
i32 = jnp.int32

D_MODEL = 1024
CHUNK = 64
N_LEFT_CHUNKS = 8
LEFT = N_LEFT_CHUNKS * CHUNK
BAND = LEFT + CHUNK
ATTN_HEADS = 8
HEAD_DIM = 64
D_ATTN = ATTN_HEADS * HEAD_DIM
MAX_REL = 256
D_CONV = D_MODEL - D_ATTN
CONV_WIDTH = 31
D_IN = 2 * D_CONV + 3 * D_ATTN + 2 * D_MODEL
N_EXPERTS = 16
N_GROUPS = 4
EXPERTS_PER_GROUP = N_EXPERTS // N_GROUPS
D_FF_EXPERT = 512
LN_EPS = 1e-5
LOG2E = math.log2(math.e)

PAIRS = [(a, b) for a in range(EXPERTS_PER_GROUP) for b in range(a + 1, EXPERTS_PER_GROUP)]
N_PAIRS = len(PAIRS)
N_BUCKETS = N_GROUPS * N_PAIRS
BUCKET_ROWS = 32

LANES = 128
SUBLANES = 8
TOKEN_TILE = 512
EXPERT_TILE = 256
HALO_ROWS = 32
CONV_ROWS = 128
HEAD_PAIRS = ATTN_HEADS // 2
AUG = D_MODEL + LANES
DMA_UNROLL = 8
GATHER_GROUPS = 4
SCATTER_TILES = (2048, 1024, 512)
VMEM_LIMIT = 56 * 1024 * 1024
NEG_BIG = -1e30


def _layer_norm(x, g, b):
    mu = jnp.mean(x, axis=-1, keepdims=True)
    xc = x - mu
    var = jnp.mean(xc * xc, axis=-1, keepdims=True)
    return xc * lax.rsqrt(var + LN_EPS) * g + b


def _sigmoid(x):
    return 1.0 / (1.0 + jnp.exp(-x))


def _row_copy(src, src_row, dst, dst_row, sem):
    return pltpu.make_async_copy(src.at[pl.ds(src_row, 1)], dst.at[pl.ds(dst_row, 1)], sem)


def _for_rows(n, body, op):
    if op == "start":
        for r in range(n):
            body(r)
    else:
        def step(r, c):
            body(r)
            return c
        lax.fori_loop(0, n, step, 0, unroll=DMA_UNROLL)


def _gather_rows(op, idx_ref, src_hbm, dst, sem):
    def body(r):
        getattr(_row_copy(src_hbm, idx_ref[0, 0, r], dst, r, sem), op)()
    _for_rows(idx_ref.shape[-1], body, op)


def _scatter_rows(op, idx_ref, src, dst_hbm, sem):
    def body(r):
        getattr(_row_copy(src, r, dst_hbm, idx_ref[0, 0, r], sem), op)()
    _for_rows(idx_ref.shape[-1], body, op)


def _gather_step(pos_ref, nxt_ref, src_hbm, buf, sem):
    i = pl.program_id(0)
    slot = lax.rem(i, 2)

    @pl.when(i == 0)
    def _():
        _gather_rows("start", pos_ref, src_hbm, buf.at[0], sem.at[0])

    @pl.when(i + 1 < pl.num_programs(0))
    def _():
        _gather_rows("start", nxt_ref, src_hbm, buf.at[1 - slot], sem.at[1 - slot])

    _gather_rows("wait", pos_ref, src_hbm, buf.at[slot], sem.at[slot])
    return slot


def _pos_specs(nt, ts):
    cur = pl.BlockSpec((1, 1, ts), lambda i: (i, 0, 0), memory_space=pltpu.SMEM)
    nxt = pl.BlockSpec((1, 1, ts), lambda i: (jnp.minimum(i + 1, nt - 1), 0, 0),
                       memory_space=pltpu.SMEM)
    return cur, nxt


def _conv_branch(u, first_of_seq, wdw_ref, bdw_ref, lcg_ref, lcb_ref, c_ref, uext, ushift):
    tm = u.shape[0]

    @pl.when(first_of_seq)
    def _():
        uext[0:HALO_ROWS, :] = jnp.zeros((HALO_ROWS, D_CONV), f32)

    @pl.when(jnp.logical_not(first_of_seq))
    def _():
        uext[0:HALO_ROWS, :] = uext[tm:tm + HALO_ROWS, :]

    uext[HALO_ROWS:HALO_ROWS + tm, :] = u
    span = tm + HALO_ROWS - SUBLANES
    for p in range(1, SUBLANES):
        ushift[p - 1, 0:span, :] = uext[p:p + span, :]
    bdw = bdw_ref[...]
    lcg = lcg_ref[...]
    lcb = lcb_ref[...]
    first_tap = HALO_ROWS - (CONV_WIDTH - 1)

    def conv_rows(rc):
        r0 = rc * CONV_ROWS
        blocks = []
        for cb in range(D_CONV // LANES):
            cl = slice(cb * LANES, (cb + 1) * LANES)
            acc = jnp.zeros((CONV_ROWS, LANES), f32)
            for w in range(CONV_WIDTH):
                off = first_tap + w
                p = off % SUBLANES
                base = r0 + off - p
                if p == 0:
                    rows = uext[base:base + CONV_ROWS, cl]
                else:
                    rows = ushift[p - 1, base:base + CONV_ROWS, cl]
                acc = acc + rows * wdw_ref[w:w + 1, cl]
            blocks.append(acc)
        y = _layer_norm(jnp.concatenate(blocks, axis=1) + bdw, lcg, lcb)
        y = y * _sigmoid(y)
        c_ref[r0:r0 + CONV_ROWS, :] = y.astype(bf16)

    return [functools.partial(conv_rows, rc) for rc in range(tm // CONV_ROWS)]


def _proj_body(tiles_per_seq, x, extra_steps, w_ref, b_ref, conv_refs, out_refs, conv_scratch):
    c_ref, q_ref, k_ref, v_ref, g_ref = out_refs
    x = x.astype(bf16)

    def seg(lo, hi):
        return jnp.dot(x, w_ref[:, lo:hi], preferred_element_type=f32) + b_ref[:, lo:hi]

    o = 0
    a = seg(o, o + D_CONV)
    gate = seg(o + D_CONV, o + 2 * D_CONV)
    first_of_seq = lax.rem(pl.program_id(0), tiles_per_seq) == 0
    conv_steps = _conv_branch(a * _sigmoid(gate), first_of_seq, *conv_refs, c_ref, *conv_scratch)
    o += 2 * D_CONV

    def q_step(o=o):
        q_ref[...] = (seg(o, o + D_ATTN) * (HEAD_DIM ** -0.5 * LOG2E)).astype(bf16)

    def copy_step(ref, o):
        ref[...] = seg(o, o + D_ATTN).astype(bf16)

    def gate_step(n, o):
        g_ref[:, n * D_ATTN:(n + 1) * D_ATTN] = _sigmoid(seg(o, o + D_ATTN)).astype(bf16)

    matmul_steps = [q_step,
                    functools.partial(copy_step, k_ref, o + D_ATTN),
                    functools.partial(copy_step, v_ref, o + 2 * D_ATTN)]
    o += 3 * D_ATTN
    matmul_steps += [functools.partial(gate_step, n, o + n * D_ATTN)
                     for n in range(2 * D_MODEL // D_ATTN)]
    for n in range(max(len(conv_steps), len(matmul_steps), len(extra_steps))):
        for steps in (conv_steps, extra_steps, matmul_steps):
            if n < len(steps):
                steps[n]()


def _proj_kernel(tiles_per_seq, x_ref, w_ref, b_ref, *rest):
    conv_refs, out_refs, conv_scratch = rest[:4], rest[4:9], rest[9:]
    _proj_body(tiles_per_seq, x_ref[...], (), w_ref, b_ref, conv_refs, out_refs, conv_scratch)


def _proj_gather_kernel(tiles_per_seq, pos_ref, nxt_ref, ys_hbm, w_ref, b_ref, *rest):
    conv_refs, x_ref, out_refs = rest[:4], rest[4], rest[5:10]
    buf, sem, conv_scratch = rest[10], rest[11], rest[12:]
    i = pl.program_id(0)
    tm = x_ref.shape[0]
    slot = lax.rem(i, 2)

    @pl.when(i == 0)
    def _():
        _gather_rows("start", pos_ref, ys_hbm, buf.at[0], sem.at[0])

    _gather_rows("wait", pos_ref, ys_hbm, buf.at[slot], sem.at[slot])

    def start_rows(lo):
        for r in range(lo, lo + tm // GATHER_GROUPS):
            _row_copy(ys_hbm, nxt_ref[0, 0, r], buf.at[1 - slot], r, sem.at[1 - slot]).start()

    start_steps = [functools.partial(start_rows, n * (tm // GATHER_GROUPS))
                   for n in range(GATHER_GROUPS)]
    x = buf[slot]
    x_ref[...] = x
    _proj_body(tiles_per_seq, x, start_steps, w_ref, b_ref, conv_refs, out_refs, conv_scratch)

    @pl.when(i == pl.num_programs(0) - 1)
    def _():
        _gather_rows("wait", nxt_ref, ys_hbm, buf.at[1 - slot], sem.at[1 - slot])


_PROJ_WIDTHS = (D_CONV, D_ATTN, D_ATTN, D_ATTN, 2 * D_MODEL)


def _proj_common(tm):
    row = lambda n: pl.BlockSpec((tm, n), lambda i: (i, 0))
    const = lambda shape: pl.BlockSpec(shape, lambda i: (0,) * len(shape))
    weight_specs = [const((D_MODEL, D_IN)), const((1, D_IN)), const((HALO_ROWS, D_CONV)),
                    const((1, D_CONV)), const((1, D_CONV)), const((1, D_CONV))]
    conv_scratch = [pltpu.VMEM((HALO_ROWS + tm, D_CONV), f32),
                    pltpu.VMEM((SUBLANES - 1, HALO_ROWS + tm, D_CONV), f32)]
    params = pltpu.CompilerParams(dimension_semantics=("arbitrary",), vmem_limit_bytes=VMEM_LIMIT)
    return row, weight_specs, conv_scratch, params


def _proj(tiles_per_seq, x, w_in, b_in, wdw, bdw, lcg, lcb):
    t = x.shape[0]
    tm = TOKEN_TILE
    row, weight_specs, conv_scratch, params = _proj_common(tm)
    return pl.pallas_call(
        functools.partial(_proj_kernel, tiles_per_seq),
        grid=(t // tm,),
        in_specs=[row(D_MODEL)] + weight_specs,
        out_specs=[row(n) for n in _PROJ_WIDTHS],
        out_shape=[jax.ShapeDtypeStruct((t, n), bf16) for n in _PROJ_WIDTHS],
        scratch_shapes=conv_scratch,
        compiler_params=params,
        name="proj",
    )(x, w_in, b_in, wdw, bdw, lcg, lcb)


def _proj_gather(tiles_per_seq, pos, ys, w_in, b_in, wdw, bdw, lcg, lcb):
    nt, _, tm = pos.shape
    t = nt * tm
    row, weight_specs, conv_scratch, params = _proj_common(tm)
    cur, nxt = _pos_specs(nt, tm)
    return pl.pallas_call(
        functools.partial(_proj_gather_kernel, tiles_per_seq),
        grid=(nt,),
        in_specs=[cur, nxt, pl.BlockSpec(memory_space=pl.ANY)] + weight_specs,
        out_specs=[row(D_MODEL)] + [row(n) for n in _PROJ_WIDTHS],
        out_shape=[jax.ShapeDtypeStruct((t, D_MODEL), f32)]
                  + [jax.ShapeDtypeStruct((t, n), bf16) for n in _PROJ_WIDTHS],
        scratch_shapes=[pltpu.VMEM((2, tm, D_MODEL), f32), pltpu.SemaphoreType.DMA((2,))]
                       + conv_scratch,
        compiler_params=params,
        name="proj_gather",
    )(pos, pos, ys, w_in, b_in, wdw, bdw, lcg, lcb)


def _route(x1, wrt_ref, br_ref, tri_ref, run_ref):
    ts = x1.shape[0]
    logits = lax.dot_general(wrt_ref[...], x1.astype(bf16), (((1,), (1,)), ((), ())),
                             preferred_element_type=f32) + br_ref[...]
    mx = jnp.max(logits, axis=0, keepdims=True)
    ex = jnp.exp(logits - mx)
    probs = ex / jnp.sum(ex, axis=0, keepdims=True)

    best = None
    for g in range(N_GROUPS):
        r = [probs[g * EXPERTS_PER_GROUP + j:g * EXPERTS_PER_GROUP + j + 1, :]
             for j in range(EXPERTS_PER_GROUP)]
        v1 = jnp.maximum(jnp.maximum(r[0], r[1]), jnp.maximum(r[2], r[3]))
        i1 = jnp.where(r[0] == v1, 0, jnp.where(r[1] == v1, 1, jnp.where(r[2] == v1, 2, 3)))
        rm = [jnp.where(i1 == j, -1.0, r[j]) for j in range(EXPERTS_PER_GROUP)]
        v2 = jnp.maximum(jnp.maximum(rm[0], rm[1]), jnp.maximum(rm[2], rm[3]))
        i2 = jnp.where(rm[0] == v2, 0, jnp.where(rm[1] == v2, 1, jnp.where(rm[2] == v2, 2, 3)))
        cand = (v1 + v2, v1, v2, i1, i2, jnp.full_like(i1, g))
        if best is None:
            best = cand
        else:
            take = cand[0] > best[0]
            best = tuple(jnp.where(take, c, b) for c, b in zip(cand, best))
    _, v1, v2, i1, i2, grp = best
    den = v1 + v2
    w1 = v1 / den
    w2 = v2 / den
    first_is_lo = i1 < i2
    lo = jnp.where(first_is_lo, i1, i2)
    hi = jnp.where(first_is_lo, i2, i1)
    w_lo = jnp.where(first_is_lo, w1, w2)
    w_hi = jnp.where(first_is_lo, w2, w1)
    pair_base = jnp.where(lo == 0, 0, jnp.where(lo == 1, 3, 5))
    bucket = grp * N_PAIRS + pair_base + (hi - lo - 1)

    rows = lax.broadcasted_iota(i32, (BUCKET_ROWS, ts), 0)
    onehot = (rows == bucket).astype(f32)
    before = jnp.dot(onehot.astype(bf16), tri_ref[...], preferred_element_type=f32)
    rank = jnp.sum(onehot * (before + run_ref[:, 0:1]), axis=0, keepdims=True)
    run_ref[...] = run_ref[...] + jnp.sum(onehot, axis=1, keepdims=True)
    return bucket, rank.astype(i32), w_lo, w_hi


def _attention(first_tile, q_ref, kbuf, vbuf, bias_ref, abuf, sbuf, pbuf, dbuf):
    ts = q_ref.shape[0]
    n_chunks = ts // CHUNK
    col = lax.broadcasted_iota(i32, (2 * CHUNK, BAND), 1)
    first_head = lax.broadcasted_iota(i32, (CHUNK, 2 * HEAD_DIM), 1) < HEAD_DIM
    lanes = [slice(hp * 2 * HEAD_DIM, (hp + 1) * 2 * HEAD_DIM) for hp in range(HEAD_PAIRS)]

    def scores(c):
        for hp in range(HEAD_PAIRS):
            q2 = q_ref[c * CHUNK:(c + 1) * CHUNK, lanes[hp]]
            k2 = kbuf[c * CHUNK:c * CHUNK + BAND, lanes[hp]]
            zero = jnp.zeros_like(q2)
            qq = jnp.concatenate([jnp.where(first_head, q2, zero),
                                  jnp.where(first_head, zero, q2)], axis=0)
            sbuf[c % 2, hp] = lax.dot_general(qq, k2, (((1,), (1,)), ((), ())),
                                              preferred_element_type=f32)

    def softmax(c):
        for hp in range(HEAD_PAIRS):
            s = sbuf[c % 2, hp] + bias_ref[hp]
            if first_tile:
                s = jnp.where(col >= LEFT - c * CHUNK, s, NEG_BIG)
            p = jnp.exp2(s - jnp.max(s, axis=-1, keepdims=True))
            dbuf[c % 2, hp] = jnp.sum(p, axis=-1, keepdims=True)
            pbuf[c % 2, hp] = p.astype(bf16)

    def values(c):
        for hp in range(HEAD_PAIRS):
            v2 = vbuf[c * CHUNK:c * CHUNK + BAND, lanes[hp]]
            o = jnp.dot(pbuf[c % 2, hp], v2, preferred_element_type=f32) / dbuf[c % 2, hp]
            o = jnp.where(first_head, o[0:CHUNK], o[CHUNK:2 * CHUNK])
            abuf[c * CHUNK:(c + 1) * CHUNK, lanes[hp]] = o.astype(bf16)

    scores(0)
    for c in range(n_chunks):
        if c + 1 < n_chunks:
            scores(c + 1)
        softmax(c)
        if c >= 1:
            values(c - 1)
    values(n_chunks - 1)


def _mix_kernel(alpha,
                c_ref, q_ref, kc_ref, kp_ref, vc_ref, vp_ref, g_ref, x_ref,
                wco_ref, bias_ref, wao_ref, wo_ref, bo_ref, l1g_ref, l1b_ref, wrt_ref, br_ref,
                xa_ref, bucket_ref, rank_ref, counts_ref,
                kbuf, vbuf, abuf, sbuf, pbuf, dbuf, tri_ref, run_ref):
    b = pl.program_id(0)
    i = pl.program_id(1)
    ts = q_ref.shape[0]

    @pl.when(jnp.logical_and(b == 0, i == 0))
    def _():
        run_ref[...] = jnp.zeros_like(run_ref)
        s = lax.broadcasted_iota(i32, (ts, ts), 0)
        t = lax.broadcasted_iota(i32, (ts, ts), 1)
        tri_ref[...] = jnp.where(s < t, 1.0, 0.0).astype(bf16)

    kbuf[0:ts, :] = kp_ref[...]
    kbuf[ts:2 * ts, :] = kc_ref[...]
    vbuf[0:ts, :] = vp_ref[...]
    vbuf[ts:2 * ts, :] = vc_ref[...]
    for first_tile in (True, False):
        @pl.when((i == 0) == first_tile)
        def _(first_tile=first_tile):
            _attention(first_tile, q_ref, kbuf, vbuf, bias_ref, abuf, sbuf, pbuf, dbuf)

    y_conv = jnp.dot(c_ref[...], wco_ref[...], preferred_element_type=f32)
    y_attn = jnp.dot(abuf[...], wao_ref[...], preferred_element_type=f32)
    merged = (g_ref[:, 0:D_MODEL] * y_conv.astype(bf16)
              + g_ref[:, D_MODEL:2 * D_MODEL] * y_attn.astype(bf16))
    mixed = jnp.dot(merged, wo_ref[...], preferred_element_type=f32) + bo_ref[...]
    x1 = _layer_norm(alpha * x_ref[...] + mixed, l1g_ref[...], l1b_ref[...])

    bucket, rank, w_lo, w_hi = _route(x1, wrt_ref, br_ref, tri_ref, run_ref)
    bucket_ref[0] = bucket
    rank_ref[0] = rank
    counts_ref[...] = run_ref[...]
    wrows = jnp.concatenate([w_lo, w_hi, jnp.zeros((LANES - 2, ts), f32)], axis=0)
    xa_ref[:, 0:D_MODEL] = x1
    xa_ref[:, D_MODEL:AUG] = wrows.T


def _mix(alpha, nb, ns, cact, q, k, v, g, x, wco, bias, wao, wo, bo, l1g, l1b, wrt, br):
    ts = TOKEN_TILE
    nst = ns // ts
    t = nb * ns
    cur = lambda n: pl.BlockSpec((ts, n), lambda b, i: (b * nst + i, 0))
    prev = lambda n: pl.BlockSpec((ts, n), lambda b, i: (jnp.maximum(b * nst + i - 1, 0), 0))
    const = lambda shape: pl.BlockSpec(shape, lambda b, i: (0,) * len(shape))
    meta = pl.BlockSpec((1, 1, ts), lambda b, i: (b * nst + i, 0, 0))
    in_specs = [
        cur(D_CONV), cur(D_ATTN), cur(D_ATTN), prev(D_ATTN), cur(D_ATTN), prev(D_ATTN),
        cur(2 * D_MODEL), cur(D_MODEL),
        const((D_CONV, D_MODEL)), const((HEAD_PAIRS, 2 * CHUNK, BAND)), const((D_ATTN, D_MODEL)),
        const((D_MODEL, D_MODEL)), const((1, D_MODEL)), const((1, D_MODEL)), const((1, D_MODEL)),
        const((N_EXPERTS, D_MODEL)), const((N_EXPERTS, 1)),
    ]
    out_specs = [
        pl.BlockSpec((ts, AUG), lambda b, i: (b * nst + i, 0)),
        meta, meta,
        pl.BlockSpec((BUCKET_ROWS, LANES), lambda b, i: (0, 0)),
    ]
    out_shape = [
        jax.ShapeDtypeStruct((t, AUG), f32),
        jax.ShapeDtypeStruct((t // ts, 1, ts), i32),
        jax.ShapeDtypeStruct((t // ts, 1, ts), i32),
        jax.ShapeDtypeStruct((BUCKET_ROWS, LANES), f32),
    ]
    scratch = [
        pltpu.VMEM((2 * ts, D_ATTN), bf16),
        pltpu.VMEM((2 * ts, D_ATTN), bf16),
        pltpu.VMEM((ts, D_ATTN), bf16),
        pltpu.VMEM((2, HEAD_PAIRS, 2 * CHUNK, BAND), f32),
        pltpu.VMEM((2, HEAD_PAIRS, 2 * CHUNK, BAND), bf16),
        pltpu.VMEM((2, HEAD_PAIRS, 2 * CHUNK, 1), f32),
        pltpu.VMEM((ts, ts), bf16),
        pltpu.VMEM((BUCKET_ROWS, LANES), f32),
    ]
    return pl.pallas_call(
        functools.partial(_mix_kernel, alpha),
        grid=(nb, nst),
        in_specs=in_specs, out_specs=out_specs, out_shape=out_shape,
        scratch_shapes=scratch,
        compiler_params=pltpu.CompilerParams(
            dimension_semantics=("arbitrary", "arbitrary"), vmem_limit_bytes=VMEM_LIMIT),
        name="mix",
    )(cact, q, k, k, v, v, g, x, wco, bias, wao, wo, bo, l1g, l1b, wrt, br)


def _scatter_kernel(pad_start_ref, pad_len_ref, used_ref, pos_ref, xa_ref, xs_hbm, zbuf, sem):
    @pl.when(pl.program_id(0) == 0)
    def _():
        zbuf[...] = jnp.zeros_like(zbuf)
        tm = zbuf.shape[0]
        n_tiles = xs_hbm.shape[0] // tm
        for op in ("start", "wait"):
            def bucket_body(b, c, op=op):
                def row_body(r, c2):
                    getattr(_row_copy(zbuf, 0, xs_hbm, pad_start_ref[b] + r, sem), op)()
                    return c2
                lax.fori_loop(0, pad_len_ref[b], row_body, 0)
                return c
            lax.fori_loop(0, N_BUCKETS, bucket_body, 0)

            def tile_body(j, c, op=op):
                dst = xs_hbm.at[pl.ds(pl.multiple_of(j * tm, tm), tm)]
                getattr(pltpu.make_async_copy(zbuf, dst, sem), op)()
                return c
            lax.fori_loop(used_ref[0], n_tiles, tile_body, 0)

    _scatter_rows("start", pos_ref, xa_ref, xs_hbm, sem)
    _scatter_rows("wait", pos_ref, xa_ref, xs_hbm, sem)


def _scatter(pad_start, pad_len, used_tiles, pos, xa, n_rows):
    ts = next(n for n in SCATTER_TILES if pos.size % n == 0)
    pos = pos.reshape(pos.size // ts, 1, ts)
    nt = pos.shape[0]
    grid_spec = pltpu.PrefetchScalarGridSpec(
        num_scalar_prefetch=3,
        grid=(nt,),
        in_specs=[pl.BlockSpec((1, 1, ts), lambda i, *_: (i, 0, 0), memory_space=pltpu.SMEM),
                  pl.BlockSpec((ts, AUG), lambda i, *_: (i, 0))],
        out_specs=pl.BlockSpec(memory_space=pl.ANY),
        scratch_shapes=[pltpu.VMEM((EXPERT_TILE, AUG), f32), pltpu.SemaphoreType.DMA(())],
    )
    return pl.pallas_call(
        _scatter_kernel,
        grid_spec=grid_spec,
        out_shape=jax.ShapeDtypeStruct((n_rows, AUG), f32),
        compiler_params=pltpu.CompilerParams(
            dimension_semantics=("arbitrary",), has_side_effects=True,
            vmem_limit_bytes=VMEM_LIMIT),
        name="scatter",
    )(pad_start, pad_len, used_tiles, pos, xa)


def _expert_kernel(alpha, elo_ref, ehi_ref, valid_ref, xs_ref, wgl_ref, wgh_ref, wdl_ref,
                   wdh_ref, g_ref, b_ref, ys_ref):
    j = pl.program_id(0)

    @pl.when(valid_ref[j] == 1)
    def _():
        x32 = xs_ref[:, 0:D_MODEL]
        x = x32.astype(bf16)
        y = alpha * x32
        for e, (wg_ref, wd_ref) in enumerate(((wgl_ref, wdl_ref), (wgh_ref, wdh_ref))):
            gu = jnp.dot(x, wg_ref[...], preferred_element_type=f32)
            gate = gu[:, 0:D_FF_EXPERT]
            up = gu[:, D_FF_EXPERT:2 * D_FF_EXPERT]
            h = (gate * _sigmoid(gate)) * up
            d = jnp.dot(h.astype(bf16), wd_ref[...], preferred_element_type=f32)
            y = y + xs_ref[:, D_MODEL + e:D_MODEL + e + 1] * d
        ys_ref[...] = _layer_norm(y, g_ref[...], b_ref[...])

    @pl.when(valid_ref[j] == 0)
    def _():
        ys_ref[...] = jnp.zeros_like(ys_ref)


def _experts(alpha, e_lo, e_hi, valid, xs, wgu, wd, g, b):
    n_rows = xs.shape[0]
    tm = EXPERT_TILE
    grid_spec = pltpu.PrefetchScalarGridSpec(
        num_scalar_prefetch=3,
        grid=(n_rows // tm,),
        in_specs=[
            pl.BlockSpec((tm, AUG), lambda j, lo, hi, ok: (j, 0)),
            pl.BlockSpec((None, D_MODEL, 2 * D_FF_EXPERT), lambda j, lo, hi, ok: (lo[j], 0, 0)),
            pl.BlockSpec((None, D_MODEL, 2 * D_FF_EXPERT), lambda j, lo, hi, ok: (hi[j], 0, 0)),
            pl.BlockSpec((None, D_FF_EXPERT, D_MODEL), lambda j, lo, hi, ok: (lo[j], 0, 0)),
            pl.BlockSpec((None, D_FF_EXPERT, D_MODEL), lambda j, lo, hi, ok: (hi[j], 0, 0)),
            pl.BlockSpec((1, D_MODEL), lambda j, lo, hi, ok: (0, 0)),
            pl.BlockSpec((1, D_MODEL), lambda j, lo, hi, ok: (0, 0)),
        ],
        out_specs=pl.BlockSpec((tm, D_MODEL), lambda j, lo, hi, ok: (j, 0)),
    )
    return pl.pallas_call(
        functools.partial(_expert_kernel, alpha),
        grid_spec=grid_spec,
        out_shape=jax.ShapeDtypeStruct((n_rows, D_MODEL), f32),
        compiler_params=pltpu.CompilerParams(
            dimension_semantics=("arbitrary",), vmem_limit_bytes=VMEM_LIMIT),
        name="experts",
    )(e_lo, e_hi, valid, xs, wgu, wgu, wd, wd, g, b)


def _gather_kernel(pos_ref, nxt_ref, ys_hbm, out_ref, buf, sem):
    slot = _gather_step(pos_ref, nxt_ref, ys_hbm, buf, sem)
    out_ref[...] = buf[slot]


def _gather(pos, ys):
    nt, _, ts = pos.shape
    cur, nxt = _pos_specs(nt, ts)
    return pl.pallas_call(
        _gather_kernel,
        grid=(nt,),
        in_specs=[cur, nxt, pl.BlockSpec(memory_space=pl.ANY)],
        out_specs=pl.BlockSpec((ts, D_MODEL), lambda i: (i, 0)),
        out_shape=jax.ShapeDtypeStruct((nt * ts, D_MODEL), f32),
        scratch_shapes=[pltpu.VMEM((2, ts, D_MODEL), f32), pltpu.SemaphoreType.DMA((2,))],
        compiler_params=pltpu.CompilerParams(
            dimension_semantics=("arbitrary",), vmem_limit_bytes=VMEM_LIMIT),
        name="gather",
    )(pos, pos, ys)


def _bucket_tables():
    lo = np.array([g * EXPERTS_PER_GROUP + a for g in range(N_GROUPS) for a, _ in PAIRS], np.int32)
    hi = np.array([g * EXPERTS_PER_GROUP + b for g in range(N_GROUPS) for _, b in PAIRS], np.int32)
    return lo, hi


def _score_bias(rel_bias):
    diag = np.arange(CHUNK - 1 + BAND)
    idx = np.clip(LEFT + CHUNK - 1 - diag, -MAX_REL, MAX_REL) + MAX_REL
    ext = (rel_bias * LOG2E)[:, idx]
    rows = [ext[:, CHUNK - 1 - i:CHUNK - 1 - i + BAND] for i in range(CHUNK)]
    return jnp.stack(rows, axis=1).reshape(HEAD_PAIRS, 2 * CHUNK, BAND)


def _table_lookup(table, idx):
    out = jnp.zeros(idx.shape, table.dtype)
    for n in range(table.shape[0]):
        out = out + jnp.where(idx == n, table[n], 0)
    return out


def kernel(x, w_in, b_in, w_dw, b_dw, ln_conv_g, ln_conv_b, w_conv_out, rel_bias, w_attn_out,
           w_o, b_o, ln1_g, ln1_b, w_router, b_router, w_gate_up, w_down, ln2_g, ln2_b):
    nb, ns, d = x.shape
    depth = w_in.shape[0]
    assert d == D_MODEL and ns % TOKEN_TILE == 0 and TOKEN_TILE >= LEFT
    alpha = (2.0 * depth) ** 0.25
    t = nb * ns
    tiles_per_seq = ns // TOKEN_TILE
    n_tiles = t // EXPERT_TILE + N_BUCKETS
    n_rows = n_tiles * EXPERT_TILE
    lo_np, hi_np = _bucket_tables()
    lo_tab, hi_tab = jnp.asarray(lo_np), jnp.asarray(hi_np)
    row = lambda a: a.reshape(1, -1)

    wrt = w_router.T.astype(bf16)
    br = b_router.reshape(N_EXPERTS, 1)
    h = x.reshape(t, d)
    pos = ys = None
    for l in range(depth):
        wdw = jnp.zeros((HALO_ROWS, D_CONV), f32).at[:CONV_WIDTH].set(w_dw[l])
        proj_weights = (w_in[l].astype(bf16), row(b_in[l]), wdw, row(b_dw[l]),
                        row(ln_conv_g[l]), row(ln_conv_b[l]))
        if l == 0:
            cact, q, k, v, g = _proj(tiles_per_seq, h, *proj_weights)
        else:
            h, cact, q, k, v, g = _proj_gather(tiles_per_seq, pos, ys, *proj_weights)
        xa, bucket, rank, counts = _mix(
            alpha, nb, ns, cact, q, k, v, g, h, w_conv_out[l].astype(bf16),
            _score_bias(rel_bias[l]), w_attn_out[l].astype(bf16), w_o[l].astype(bf16),
            row(b_o[l]), row(ln1_g[l]), row(ln1_b[l]), wrt, br)

        cnt = counts[:N_BUCKETS, 0].astype(i32)
        tiles = (cnt + EXPERT_TILE - 1) // EXPERT_TILE
        tile_end = jnp.cumsum(tiles)
        row_start = (tile_end - tiles) * EXPERT_TILE
        pos = _table_lookup(row_start, bucket) + rank
        tile_ids = jnp.arange(n_tiles, dtype=i32)
        tile_bucket = jnp.minimum(
            jnp.sum((tile_ids[:, None] >= tile_end[None, :]).astype(i32), axis=1), N_BUCKETS - 1)
        valid = (tile_ids < tile_end[-1]).astype(i32)

        xs = _scatter(row_start + cnt, tiles * EXPERT_TILE - cnt, tile_end[-1:], pos, xa, n_rows)
        ys = _experts(alpha, _table_lookup(lo_tab, tile_bucket), _table_lookup(hi_tab, tile_bucket),
                      valid, xs, w_gate_up[l].astype(bf16), w_down[l].astype(bf16),
                      row(ln2_g[l]), row(ln2_b[l]))
    return _gather(pos, ys).reshape(nb, ns, d)
```
